```python
import math, functools
import jax, jax.numpy as jnp
from jax import lax
import numpy as np

D_MODEL = 2048
BATCH = 16
SEQ = 256
DEPTH = 4
DEC_BATCH = 4
DEC_SEQ = 4096
PAST_LEN = 512

GRID_W = 64
POOL_WIDTH = D_MODEL // 4
ATTN_WIDTH = D_MODEL // 2
SSM_INNER = D_MODEL // 4
D_MIX = POOL_WIDTH + ATTN_WIDTH + SSM_INNER
POOL_WINDOWS = (2, 4, 8, 16)
POOL_GROUPS = len(POOL_WINDOWS)
POOL_CH = POOL_WIDTH // POOL_GROUPS
HEAD_DIM = 128
N_HEADS = ATTN_WIDTH // HEAD_DIM
N_KV_HEADS = 2
GQA = N_HEADS // N_KV_HEADS
KV_WIDTH = N_KV_HEADS * HEAD_DIM
ATTN_WINDOW = 128
ATTN_BLOCK = 128
ATTN_SCALE = HEAD_DIM ** -0.5
ROPE_PAIRS = HEAD_DIM // 4
ROPE_BASE = 10000.0
SSM_HEADDIM = 64
SSM_HEADS = SSM_INNER // SSM_HEADDIM
SSM_GROUPS = 2
SSM_STATE = 128
SSM_CONV = 5
SSM_CHUNK = 128
CONV_DIM = SSM_INNER + 2 * SSM_GROUPS * SSM_STATE
D_FF = 7168
N_EXPERTS = 8
TOP_K = 2
D_FF_EXPERT = 7168
MOE_BLOCK = 512
N_DENSE = (DEPTH + 1) // 2
N_MOE = DEPTH // 2
PROJ_SIZES = (POOL_WIDTH, ATTN_WIDTH, KV_WIDTH, KV_WIDTH, SSM_INNER, CONV_DIM, 2 * SSM_HEADS)
D_IN = sum(PROJ_SIZES)
RMS_EPS = 1e-6

kernel_name = 'hybrid_pool_swa_ssd_dit_step'

f32 = jnp.float32


def rms_norm(x, g):
    xf = x.astype(f32)
    y = xf * lax.rsqrt(jnp.mean(jnp.square(xf), axis=-1, keepdims=True) + RMS_EPS)
    return (y * g.astype(f32)).astype(x.dtype)


def adaln_mod(cond, w, b):
    return (jax.nn.silu(cond) @ w + b)[:, None, :]


def multiscale_pool(u, pool_w, pool_scale):
    B, L, _ = u.shape
    ug = u.astype(f32).reshape(B, L, POOL_GROUPS, POOL_CH)
    cs = jnp.pad(jnp.cumsum(ug, axis=1), ((0, 0), (1, 0), (0, 0), (0, 0)))
    win = jnp.array(POOL_WINDOWS, dtype=jnp.int32)
    t = jnp.arange(L, dtype=jnp.int32)[:, None]
    lo = jnp.clip(t - win // 2, 0, L)
    hi = jnp.clip(t - win // 2 + win, 0, L)
    g = jnp.arange(POOL_GROUPS)[None, :]
    mean = (cs[:, hi, g] - cs[:, lo, g]) / (hi - lo).astype(f32)[None, :, :, None]
    y = jnp.einsum('blgc,gcd->blgd', mean - ug, pool_w.astype(f32))
    y = y * pool_scale.astype(f32).reshape(POOL_GROUPS, POOL_CH)
    return y.reshape(B, L, POOL_WIDTH).astype(u.dtype)


def axial_rope(L):
    rows = L // GRID_W
    row = jnp.repeat(jnp.arange(rows), GRID_W).astype(f32)
    col = jnp.tile(jnp.arange(GRID_W), rows).astype(f32)
    inv = ROPE_BASE ** (-jnp.arange(ROPE_PAIRS, dtype=f32) / ROPE_PAIRS)
    ang = jnp.stack([row[:, None] * inv, col[:, None] * inv], axis=1)
    return jnp.cos(ang), jnp.sin(ang)


def apply_rope(x, cos, sin):
    shp = x.shape
    xr = x.astype(f32).reshape(shp[:-1] + (2, 2, ROPE_PAIRS))
    bshape = (shp[1],) + (1,) * (x.ndim - 3) + (2, ROPE_PAIRS)
    c = cos.reshape(bshape)
    s = sin.reshape(bshape)
    x1, x2 = xr[..., 0, :], xr[..., 1, :]
    out = jnp.stack([x1 * c - x2 * s, x2 * c + x1 * s], axis=-2)
    return out.reshape(shp).astype(x.dtype)


def softmax_with_sink(s, sink):
    sk = sink[None, :, :, None, None]
    m = jnp.maximum(jnp.max(s, axis=-1, keepdims=True), sk)
    e = jnp.exp(s - m)
    return e / (jnp.sum(e, axis=-1, keepdims=True) + jnp.exp(sk - m))


def context_attention(q, k, v, sink):
    B, S = q.shape[:2]
    nb = S // ATTN_BLOCK
    qb = jnp.moveaxis(q.reshape(B, nb, ATTN_BLOCK, N_KV_HEADS, GQA, HEAD_DIM), 1, 0)
    sk = sink.astype(f32).reshape(N_KV_HEADS, GQA)

    def block(qi):
        s = jnp.einsum('bqkgd,bskd->bkgqs', qi, k, preferred_element_type=f32) * ATTN_SCALE
        p = softmax_with_sink(s, sk).astype(v.dtype)
        return jnp.einsum('bkgqs,bskd->bqkgd', p, v)

    o = lax.map(block, qb)
    return jnp.moveaxis(o, 0, 1).reshape(B, S, N_KV_HEADS, GQA, HEAD_DIM)


def latent_attention(q, k, v, ctx_k, ctx_v, sink):
    B, L = q.shape[:2]
    nb = L // ATTN_BLOCK
    span = ATTN_BLOCK + 2 * ATTN_WINDOW
    pad = ((0, 0), (ATTN_WINDOW, ATTN_WINDOW), (0, 0), (0, 0))
    kp = jnp.pad(k, pad)
    vp = jnp.pad(v, pad)
    qb = jnp.moveaxis(q.reshape(B, nb, ATTN_BLOCK, N_KV_HEADS, GQA, HEAD_DIM), 1, 0)
    sk = sink.astype(f32).reshape(N_KV_HEADS, GQA)
    a_idx = jnp.arange(ATTN_BLOCK)[:, None]
    b_idx = jnp.arange(span)[None, :]
    band = (b_idx >= a_idx) & (b_idx <= a_idx + 2 * ATTN_WINDOW)

    def block(args):
        n, qi = args
        start = n * ATTN_BLOCK
        kw = lax.dynamic_slice_in_dim(kp, start, span, axis=1)
        vw = lax.dynamic_slice_in_dim(vp, start, span, axis=1)
        pos = start - ATTN_WINDOW + b_idx
        mask = band & (pos >= 0) & (pos < L)
        s_win = jnp.einsum('bqkgd,bskd->bkgqs', qi, kw, preferred_element_type=f32) * ATTN_SCALE
        s_win = jnp.where(mask, s_win, -jnp.inf)
        s_ctx = jnp.einsum('bqkgd,bskd->bkgqs', qi, ctx_k, preferred_element_type=f32) * ATTN_SCALE
        p = softmax_with_sink(jnp.concatenate([s_win, s_ctx], axis=-1), sk).astype(v.dtype)
        o_win = jnp.einsum('bkgqs,bskd->bqkgd', p[..., :span], vw)
        o_ctx = jnp.einsum('bkgqs,bskd->bqkgd', p[..., span:], ctx_v.astype(v.dtype))
        return o_win + o_ctx

    o = lax.map(block, (jnp.arange(nb), qb))
    return jnp.moveaxis(o, 0, 1).reshape(B, L, N_KV_HEADS, GQA, HEAD_DIM)


def depthwise_conv(u, w, b):
    y = lax.conv_general_dilated(u, w[:, None, :].astype(u.dtype), window_strides=(1,),
                                 padding=[(SSM_CONV // 2, SSM_CONV // 2)],
                                 dimension_numbers=('NWC', 'WIO', 'NWC'),
                                 feature_group_count=u.shape[-1])
    return y + b.astype(u.dtype)


def ssd_chunked(x, dt, A, Bm, Cm, h0):
    b, L, H, P = x.shape
    G, N = Bm.shape[2], Bm.shape[3]
    J = H // G
    Q = SSM_CHUNK
    nc = L // Q
    a = (dt * A).reshape(b, nc, Q, G, J)
    xd = (x.astype(f32) * dt[..., None]).reshape(b, nc, Q, G, J, P)
    Bc = Bm.astype(f32).reshape(b, nc, Q, G, N)
    Cc = Cm.astype(f32).reshape(b, nc, Q, G, N)
    a_cs = jnp.cumsum(a, axis=2)
    causal = jnp.tril(jnp.ones((Q, Q), bool))[:, :, None, None]
    seg = jnp.where(causal, a_cs[:, :, :, None] - a_cs[:, :, None, :], -jnp.inf)
    cb = jnp.einsum('bclgn,bcsgn->bclsg', Cc, Bc)
    y_diag = jnp.einsum('bclsg,bclsgj,bcsgjp->bclgjp', cb, jnp.exp(seg), xd)
    decay_end = jnp.exp(a_cs[:, :, -1:] - a_cs)
    states = jnp.einsum('bcsgn,bcsgj,bcsgjp->bcgjpn', Bc, decay_end, xd)
    chunk_decay = jnp.exp(a_cs[:, :, -1])

    def step(h, inp):
        st, dec = inp
        return h * dec[..., None, None] + st, h

    h_last, h_in = lax.scan(step, h0.astype(f32).reshape(b, G, J, P, N),
                            (jnp.moveaxis(states, 1, 0), jnp.moveaxis(chunk_decay, 1, 0)))
    h_in = jnp.moveaxis(h_in, 0, 1)
    y_off = jnp.einsum('bclgn,bcgjpn,bclgj->bclgjp', Cc, h_in, jnp.exp(a_cs))
    return (y_diag + y_off).reshape(b, L, H, P), h_last.reshape(b, H, P, N)


def token_mix(h, w_in, w_out, pool_w, pool_scale, attn_sink, conv_w, conv_b, dt_bias, a_log, ssm_d, ssm_norm,
              ctx_k, ctx_v, ssm_h0):
    B, L, _ = h.shape
    cuts = np.cumsum(PROJ_SIZES)[:-1].tolist()
    u, q, k, v, z, xbc, dt_raw = jnp.split(h @ w_in, cuts, axis=-1)
    y_pool = multiscale_pool(u, pool_w, pool_scale)
    q = q.reshape(B, L, N_KV_HEADS, GQA, HEAD_DIM)
    k = k.reshape(B, L, N_KV_HEADS, HEAD_DIM)
    v = v.reshape(B, L, N_KV_HEADS, HEAD_DIM)
    if ctx_k is None:
        o = context_attention(q, k, v, attn_sink)
    else:
        cos, sin = axial_rope(L)
        k = apply_rope(k, cos, sin)
        o = latent_attention(apply_rope(q, cos, sin), k, v, ctx_k, ctx_v, attn_sink)
    xbc = jax.nn.silu(depthwise_conv(xbc, conv_w, conv_b))
    xs, b_in, c_in = jnp.split(xbc, [SSM_INNER, SSM_INNER + SSM_GROUPS * SSM_STATE], axis=-1)
    xs = xs.reshape(B, L, SSM_HEADS, SSM_HEADDIM)
    b_in = b_in.reshape(B, L, SSM_GROUPS, SSM_STATE)
    c_in = c_in.reshape(B, L, SSM_GROUPS, SSM_STATE)
    dt = jax.nn.softplus(dt_raw.astype(f32).reshape(B, L, 2, SSM_HEADS) + dt_bias.astype(f32))
    A = -jnp.exp(a_log.astype(f32))
    if ssm_h0 is None:
        ssm_h0 = jnp.zeros((B, 2, SSM_HEADS, SSM_HEADDIM, SSM_STATE), f32)
    y_f, h_f = ssd_chunked(xs, dt[:, :, 0], A[0], b_in, c_in, ssm_h0[:, 0])
    y_b, h_b = ssd_chunked(jnp.flip(xs, 1), jnp.flip(dt[:, :, 1], 1), A[1], jnp.flip(b_in, 1),
                           jnp.flip(c_in, 1), ssm_h0[:, 1])
    y = y_f + jnp.flip(y_b, 1) + xs.astype(f32) * ssm_d.astype(f32)[:, None]
    y = y.reshape(B, L, SSM_INNER) * jax.nn.silu(z.astype(f32))
    y_ssm = rms_norm(y, ssm_norm).astype(h.dtype)
    y_mix = jnp.concatenate([y_pool, o.reshape(B, L, ATTN_WIDTH).astype(h.dtype), y_ssm], axis=-1) @ w_out
    return y_mix, k, v, jnp.stack([h_f, h_b], axis=1)


def swiglu(h, w1, w3, w2):
    return (jax.nn.silu(h @ w1) * (h @ w3)) @ w2


def moe_swiglu(h, router_w, router_b, w1, w3, w2):
    T, D = h.shape
    TK = T * TOP_K
    logits = h.astype(f32) @ router_w.astype(f32) + router_b.astype(f32)
    top_logit, top_e = lax.top_k(logits, TOP_K)
    gate = jax.nn.softmax(top_logit, axis=-1).astype(h.dtype)
    flat_e = top_e.reshape(-1)
    flat_tok = jnp.arange(TK, dtype=jnp.int32) // TOP_K
    flat_gate = gate.reshape(-1)
    order = jnp.argsort(flat_e)
    sorted_e = flat_e[order]
    counts = jnp.bincount(flat_e, length=N_EXPERTS)
    starts = jnp.cumsum(counts) - counts
    padded = (counts + MOE_BLOCK - 1) // MOE_BLOCK * MOE_BLOCK
    pad_ends = jnp.cumsum(padded)
    pad_starts = pad_ends - padded
    dest = pad_starts[sorted_e] + jnp.arange(TK) - starts[sorted_e]
    n_blocks = -(-TK // MOE_BLOCK) + N_EXPERTS
    rows = n_blocks * MOE_BLOCK
    row_tok = jnp.zeros((rows,), jnp.int32).at[dest].set(flat_tok[order])
    row_gate = jnp.zeros((rows,), h.dtype).at[dest].set(flat_gate[order])
    block_e = jnp.searchsorted(pad_ends, jnp.arange(n_blocks) * MOE_BLOCK, side='right')
    block_e = jnp.minimum(block_e, N_EXPERTS - 1)

    def run_block(args):
        tok, e = args
        xb = h[tok]
        return (jax.nn.silu(xb @ w1[e]) * (xb @ w3[e])) @ w2[e]

    out = lax.map(run_block, (row_tok.reshape(n_blocks, MOE_BLOCK), block_e))
    out = out.reshape(rows, D) * row_gate[:, None]
    return jnp.zeros_like(h).at[row_tok].add(out)


def trunk_layer(x, mod, mix_w, norms, ffn, ctx_k, ctx_v, ssm_h0):
    shift1, scale1, gate1, shift2, scale2, gate2 = jnp.split(mod, 6, axis=-1)
    pre1, post1, pre2, post2 = norms
    h = rms_norm(x, pre1) * (1 + scale1) + shift1
    mix, k, v, st = token_mix(h, *mix_w, ctx_k, ctx_v, ssm_h0)
    x = x + gate1 * rms_norm(mix, post1)
    h = rms_norm(x, pre2) * (1 + scale2) + shift2
    B, L, D = h.shape
    f = ffn(h.reshape(B * L, D)).reshape(B, L, D)
    x = x + gate2 * rms_norm(f, post2)
    return x, k, v, st


def setup_inputs(seed: int = 0) -> dict:
    key = jax.random.key(seed)
    ks = jax.random.split(key, 40)

    def nrm(i, shape, scale):
        return jax.random.normal(ks[i], shape, f32) * scale

    dt0 = jnp.exp(jax.random.uniform(ks[30], (DEPTH, 2, SSM_HEADS), f32,
                                     minval=math.log(1e-3), maxval=math.log(1e-1)))
    return {
        'x_prompt': nrm(0, (BATCH, SEQ, D_MODEL), 1.0),
        'x_sample': nrm(1, (DEC_BATCH, DEC_SEQ, D_MODEL), 1.0),
        'cache_k': nrm(2, (DEC_BATCH, DEPTH, PAST_LEN, N_KV_HEADS, HEAD_DIM), 1.0),
        'cache_v': nrm(3, (DEC_BATCH, DEPTH, PAST_LEN, N_KV_HEADS, HEAD_DIM), 1.0),
        'state_ssm': nrm(4, (DEC_BATCH, DEPTH, 2, SSM_HEADS, SSM_HEADDIM, SSM_STATE), 0.5),
        'c': nrm(5, (DEC_BATCH, D_MODEL), 1.0),
        'c_ctx': nrm(6, (D_MODEL,), 1.0),
        'w_ada': nrm(7, (DEPTH, D_MODEL, 6 * D_MODEL), 0.5 * D_MODEL ** -0.5),
        'b_ada': nrm(8, (DEPTH, 6 * D_MODEL), 0.01),
        'norm_mix_pre': 1.0 + nrm(9, (DEPTH, D_MODEL), 0.01),
        'norm_mix_post': 1.0 + nrm(10, (DEPTH, D_MODEL), 0.01),
        'norm_ffn_pre': 1.0 + nrm(11, (DEPTH, D_MODEL), 0.01),
        'norm_ffn_post': 1.0 + nrm(12, (DEPTH, D_MODEL), 0.01),
        'w_in': nrm(13, (DEPTH, D_MODEL, D_IN), D_MODEL ** -0.5),
        'w_out': nrm(14, (DEPTH, D_MIX, D_MODEL), D_MIX ** -0.5),
        'pool_w': nrm(15, (DEPTH, POOL_GROUPS, POOL_CH, POOL_CH), POOL_CH ** -0.5),
        'pool_scale': 1.0 + nrm(16, (DEPTH, POOL_WIDTH), 0.1),
        'attn_sink': nrm(17, (DEPTH, N_HEADS), 0.5),
        'conv_w': nrm(18, (DEPTH, SSM_CONV, CONV_DIM), SSM_CONV ** -0.5),
        'conv_b': nrm(19, (DEPTH, CONV_DIM), 0.01),
        'dt_bias': dt0 + jnp.log(-jnp.expm1(-dt0)),
        'a_log': jnp.log(jax.random.uniform(ks[31], (DEPTH, 2, SSM_HEADS), f32, minval=1.0, maxval=16.0)),
        'ssm_d': 1.0 + nrm(20, (DEPTH, SSM_HEADS), 0.1),
        'ssm_norm': 1.0 + nrm(21, (DEPTH, SSM_INNER), 0.01),
        'ffn_w1': nrm(22, (N_DENSE, D_MODEL, D_FF), D_MODEL ** -0.5),
        'ffn_w3': nrm(23, (N_DENSE, D_MODEL, D_FF), D_MODEL ** -0.5),
        'ffn_w2': nrm(24, (N_DENSE, D_FF, D_MODEL), D_FF ** -0.5),
        'router_w': nrm(25, (N_MOE, D_MODEL, N_EXPERTS), D_MODEL ** -0.5),
        'router_b': nrm(26, (N_MOE, N_EXPERTS), 0.01),
        'moe_w1': nrm(27, (N_MOE, N_EXPERTS, D_MODEL, D_FF_EXPERT), D_MODEL ** -0.5),
        'moe_w3': nrm(28, (N_MOE, N_EXPERTS, D_MODEL, D_FF_EXPERT), D_MODEL ** -0.5),
        'moe_w2': nrm(29, (N_MOE, N_EXPERTS, D_FF_EXPERT, D_MODEL), D_FF_EXPERT ** -0.5),
    }


def reference(x_prompt, x_sample, cache_k, cache_v, state_ssm, c, c_ctx, w_ada, b_ada,
              norm_mix_pre, norm_mix_post, norm_ffn_pre, norm_ffn_post, w_in, w_out, pool_w, pool_scale,
              attn_sink, conv_w, conv_b, dt_bias, a_log, ssm_d, ssm_norm, ffn_w1, ffn_w3, ffn_w2,
              router_w, router_b, moe_w1, moe_w3, moe_w2):
    y_p, y_s = x_prompt, x_sample
    ks, vs, hs = [], [], []
    for l in range(DEPTH):
        mix_w = (w_in[l], w_out[l], pool_w[l], pool_scale[l], attn_sink[l], conv_w[l], conv_b[l],
                 dt_bias[l], a_log[l], ssm_d[l], ssm_norm[l])
        norms = (norm_mix_pre[l], norm_mix_post[l], norm_ffn_pre[l], norm_ffn_post[l])
        i = l // 2
        if l % 2 == 0:
            ffn = functools.partial(swiglu, w1=ffn_w1[i], w3=ffn_w3[i], w2=ffn_w2[i])
        else:
            ffn = functools.partial(moe_swiglu, router_w=router_w[i], router_b=router_b[i],
                                    w1=moe_w1[i], w3=moe_w3[i], w2=moe_w2[i])
        mod_p = adaln_mod(c_ctx[None, :], w_ada[l], b_ada[l])
        mod_s = adaln_mod(c, w_ada[l], b_ada[l])
        y_p, k_l, v_l, h_l = trunk_layer(y_p, mod_p, mix_w, norms, ffn, None, None, None)
        ks.append(k_l)
        vs.append(v_l)
        hs.append(h_l)
        y_s, _, _, _ = trunk_layer(y_s, mod_s, mix_w, norms, ffn, cache_k[:, l], cache_v[:, l], state_ssm[:, l])
    new_cache_k = jnp.stack(ks, axis=1)
    new_cache_v = jnp.stack(vs, axis=1)
    new_state_ssm = jnp.stack(hs, axis=1).astype(x_prompt.dtype)
    return (y_p, y_s, new_cache_k, new_cache_v, new_state_ssm)
```

```python
import functools
import math

import numpy as np
import jax
import jax.numpy as jnp
from jax import lax
from jax.experimental import pallas as pl
from jax.experimental.pallas import tpu as pltpu

D_MODEL = 2048
BATCH = 16
SEQ = 256
DEPTH = 4
DEC_BATCH = 4
DEC_SEQ = 4096
PAST_LEN = 512
GRID_W = 64
POOL_WIDTH = D_MODEL // 4
ATTN_WIDTH = D_MODEL // 2
SSM_INNER = D_MODEL // 4
D_MIX = POOL_WIDTH + ATTN_WIDTH + SSM_INNER
POOL_WINDOWS = (2, 4, 8, 16)
POOL_GROUPS = len(POOL_WINDOWS)
POOL_CH = POOL_WIDTH // POOL_GROUPS
HEAD_DIM = 128
N_HEADS = ATTN_WIDTH // HEAD_DIM
N_KV_HEADS = 2
GQA = N_HEADS // N_KV_HEADS
KV_WIDTH = N_KV_HEADS * HEAD_DIM
ATTN_WINDOW = 128
ATTN_BLOCK = 128
ATTN_SCALE = HEAD_DIM ** -0.5
ROPE_PAIRS = HEAD_DIM // 4
ROPE_BASE = 10000.0
SSM_HEADDIM = 64
SSM_HEADS = SSM_INNER // SSM_HEADDIM
SSM_GROUPS = 2
SSM_STATE = 128
SSM_CONV = 5
SSM_CHUNK = 128
CONV_DIM = SSM_INNER + 2 * SSM_GROUPS * SSM_STATE
D_FF = 7168
N_EXPERTS = 8
TOP_K = 2
PROJ_SIZES = (POOL_WIDTH, ATTN_WIDTH, KV_WIDTH, KV_WIDTH, SSM_INNER, CONV_DIM, 2 * SSM_HEADS)
RMS_EPS = 1e-6

N_CTX_TOK = BATCH * SEQ
N_LAT_TOK = DEC_BATCH * DEC_SEQ
N_TOK = N_CTX_TOK + N_LAT_TOK

V7X_VMEM_LIMIT = 56 * 1024 * 1024

FFN_TM = 1024
FFN_TF = 256
MOE_TM = 1024

f32 = jnp.float32
bf16 = jnp.bfloat16


def _swiglu_step(h, w1_ref, w3_ref, w2_ref):
    a = jnp.dot(h, w1_ref[...].astype(bf16), preferred_element_type=f32)
    b = jnp.dot(h, w3_ref[...].astype(bf16), preferred_element_type=f32)
    g = (a * jax.nn.sigmoid(a) * b).astype(bf16)
    return jnp.dot(g, w2_ref[...].astype(bf16), preferred_element_type=f32)


def _ffn_kernel(h_ref, w1_ref, w3_ref, w2_ref, o_ref):
    j = pl.program_id(1)
    p = _swiglu_step(h_ref[...], w1_ref, w3_ref, w2_ref)

    @pl.when(j == 0)
    def _():
        o_ref[...] = p

    @pl.when(j > 0)
    def _():
        o_ref[...] += p


def _ffn(h, w1, w3, w2):
    T, D = h.shape
    F = w1.shape[1]
    return pl.pallas_call(
        _ffn_kernel,
        grid=(T // FFN_TM, F // FFN_TF),
        in_specs=[
            pl.BlockSpec((FFN_TM, D), lambda i, j: (i, 0)),
            pl.BlockSpec((D, FFN_TF), lambda i, j: (0, j)),
            pl.BlockSpec((D, FFN_TF), lambda i, j: (0, j)),
            pl.BlockSpec((FFN_TF, D), lambda i, j: (j, 0)),
        ],
        out_specs=pl.BlockSpec((FFN_TM, D), lambda i, j: (i, 0)),
        out_shape=jax.ShapeDtypeStruct((T, D), f32),
        compiler_params=pltpu.CompilerParams(
            dimension_semantics=("arbitrary", "arbitrary"),
            vmem_limit_bytes=V7X_VMEM_LIMIT),
        name="ffn_swiglu",
    )(h, w1, w3, w2)


def _moe_kernel(be_ref, nv_ref, x_ref, w1_ref, w3_ref, w2_ref, o_ref):
    i = pl.program_id(0)
    j = pl.program_id(1)

    @pl.when(i < nv_ref[0])
    def _():
        p = _swiglu_step(x_ref[...], w1_ref, w3_ref, w2_ref)

        @pl.when(j == 0)
        def _():
            o_ref[...] = p

        @pl.when(j > 0)
        def _():
            o_ref[...] += p

    @pl.when(jnp.logical_and(i >= nv_ref[0], j == 0))
    def _():
        o_ref[...] = jnp.zeros_like(o_ref)


def _moe_experts(xb, block_e, n_valid, w1, w3, w2):
    R, D = xb.shape
    F = w1.shape[2]
    nf = F // FFN_TF
    nb = R // MOE_TM

    def jeff(i, j, nv):
        return jnp.where(i < nv[0], j, nf - 1)

    return pl.pallas_call(
        _moe_kernel,
        grid_spec=pltpu.PrefetchScalarGridSpec(
            num_scalar_prefetch=2,
            grid=(nb, nf),
            in_specs=[
                pl.BlockSpec((MOE_TM, D), lambda i, j, be, nv: (i, 0)),
                pl.BlockSpec((None, D, FFN_TF), lambda i, j, be, nv: (be[i], 0, jeff(i, j, nv))),
                pl.BlockSpec((None, D, FFN_TF), lambda i, j, be, nv: (be[i], 0, jeff(i, j, nv))),
                pl.BlockSpec((None, FFN_TF, D), lambda i, j, be, nv: (be[i], jeff(i, j, nv), 0)),
            ],
            out_specs=pl.BlockSpec((MOE_TM, D), lambda i, j, be, nv: (i, 0)),
        ),
        out_shape=jax.ShapeDtypeStruct((R, D), f32),
        compiler_params=pltpu.CompilerParams(
            dimension_semantics=("arbitrary", "arbitrary"),
            vmem_limit_bytes=V7X_VMEM_LIMIT),
        name="moe_swiglu",
    )(block_e, n_valid, xb, w1, w3, w2)


def _moe(h, router_w, router_b, w1, w3, w2):
    T, D = h.shape
    TK = T * TOP_K
    logits = h @ router_w + router_b
    top_logit, top_e = lax.top_k(logits, TOP_K)
    gate = jax.nn.softmax(top_logit, axis=-1)
    flat_e = top_e.reshape(-1)
    flat_tok = jnp.arange(TK, dtype=jnp.int32) // TOP_K
    flat_gate = gate.reshape(-1)
    order = jnp.argsort(flat_e)
    sorted_e = flat_e[order]
    counts = jnp.bincount(flat_e, length=N_EXPERTS)
    starts = jnp.cumsum(counts) - counts
    padded = (counts + MOE_TM - 1) // MOE_TM * MOE_TM
    pad_ends = jnp.cumsum(padded)
    pad_starts = pad_ends - padded
    dest = pad_starts[sorted_e] + jnp.arange(TK) - starts[sorted_e]
    n_blocks = TK // MOE_TM + N_EXPERTS
    rows = n_blocks * MOE_TM
    row_tok = jnp.zeros((rows,), jnp.int32).at[dest].set(flat_tok[order])
    row_gate = jnp.zeros((rows,), f32).at[dest].set(flat_gate[order])
    n_valid = (pad_ends[-1] // MOE_TM).astype(jnp.int32)
    block_e = jnp.searchsorted(pad_ends, jnp.arange(n_blocks) * MOE_TM, side='right')
    block_e = jnp.minimum(block_e, N_EXPERTS - 1).astype(jnp.int32)
    last_e = block_e[jnp.maximum(n_valid - 1, 0)]
    block_e = jnp.where(jnp.arange(n_blocks) < n_valid, block_e, last_e)
    xb = h.astype(bf16)[row_tok]
    out = _moe_experts(xb, block_e, n_valid.reshape(1), w1, w3, w2)
    out = out * row_gate[:, None]
    return jnp.zeros_like(h).at[row_tok].add(out)


def _rms_norm(x, g):
    return x * lax.rsqrt(jnp.mean(jnp.square(x), axis=-1, keepdims=True) + RMS_EPS) * g


def _multiscale_pool(u, pool_w, pool_scale):
    B, L, _ = u.shape
    ug = u.reshape(B, L, POOL_GROUPS, POOL_CH)
    cs = jnp.pad(jnp.cumsum(ug, axis=1), ((0, 0), (1, 0), (0, 0), (0, 0)))
    win = jnp.array(POOL_WINDOWS, dtype=jnp.int32)
    t = jnp.arange(L, dtype=jnp.int32)[:, None]
    lo = jnp.clip(t - win // 2, 0, L)
    hi = jnp.clip(t - win // 2 + win, 0, L)
    g = jnp.arange(POOL_GROUPS)[None, :]
    mean = (cs[:, hi, g] - cs[:, lo, g]) / (hi - lo).astype(f32)[None, :, :, None]
    y = jnp.einsum('blgc,gcd->blgd', mean - ug, pool_w)
    y = y * pool_scale.reshape(POOL_GROUPS, POOL_CH)
    return y.reshape(B, L, POOL_WIDTH)


def _axial_rope(L):
    rows = L // GRID_W
    row = jnp.repeat(jnp.arange(rows), GRID_W).astype(f32)
    col = jnp.tile(jnp.arange(GRID_W), rows).astype(f32)
    inv = ROPE_BASE ** (-jnp.arange(ROPE_PAIRS, dtype=f32) / ROPE_PAIRS)
    ang = jnp.stack([row[:, None] * inv, col[:, None] * inv], axis=1)
    return jnp.cos(ang), jnp.sin(ang)


def _apply_rope(x, cos, sin):
    shp = x.shape
    xr = x.reshape(shp[:-1] + (2, 2, ROPE_PAIRS))
    bshape = (shp[1],) + (1,) * (x.ndim - 3) + (2, ROPE_PAIRS)
    c = cos.reshape(bshape)
    s = sin.reshape(bshape)
    x1, x2 = xr[..., 0, :], xr[..., 1, :]
    out = jnp.stack([x1 * c - x2 * s, x2 * c + x1 * s], axis=-2)
    return out.reshape(shp)


def _softmax_with_sink(s, sink):
    sk = sink[None, :, :, None, None]
    m = jnp.maximum(jnp.max(s, axis=-1, keepdims=True), sk)
    e = jnp.exp(s - m)
    return e / (jnp.sum(e, axis=-1, keepdims=True) + jnp.exp(sk - m))


def _context_attention(q, k, v, sink):
    sk = sink.reshape(N_KV_HEADS, GQA)
    s = jnp.einsum('bqkgd,bskd->bkgqs', q, k, preferred_element_type=f32) * ATTN_SCALE
    p = _softmax_with_sink(s, sk)
    return jnp.einsum('bkgqs,bskd->bqkgd', p, v)


def _latent_attention(q, k, v, ctx_k, ctx_v, sink):
    B, L = q.shape[:2]
    nb = L // ATTN_BLOCK
    span = ATTN_BLOCK + 2 * ATTN_WINDOW
    pad = ((0, 0), (ATTN_WINDOW, ATTN_WINDOW), (0, 0), (0, 0))
    kp = jnp.pad(k, pad)
    vp = jnp.pad(v, pad)
    qb = jnp.moveaxis(q.reshape(B, nb, ATTN_BLOCK, N_KV_HEADS, GQA, HEAD_DIM), 1, 0)
    sk = sink.reshape(N_KV_HEADS, GQA)
    a_idx = jnp.arange(ATTN_BLOCK)[:, None]
    b_idx = jnp.arange(span)[None, :]
    band = (b_idx >= a_idx) & (b_idx <= a_idx + 2 * ATTN_WINDOW)

    def block(args):
        n, qi = args
        start = n * ATTN_BLOCK
        kw = lax.dynamic_slice_in_dim(kp, start, span, axis=1)
        vw = lax.dynamic_slice_in_dim(vp, start, span, axis=1)
        pos = start - ATTN_WINDOW + b_idx
        mask = band & (pos >= 0) & (pos < L)
        s_win = jnp.einsum('bqkgd,bskd->bkgqs', qi, kw, preferred_element_type=f32) * ATTN_SCALE
        s_win = jnp.where(mask, s_win, -jnp.inf)
        s_ctx = jnp.einsum('bqkgd,bskd->bkgqs', qi, ctx_k, preferred_element_type=f32) * ATTN_SCALE
        p = _softmax_with_sink(jnp.concatenate([s_win, s_ctx], axis=-1), sk)
        o_win = jnp.einsum('bkgqs,bskd->bqkgd', p[..., :span], vw)
        o_ctx = jnp.einsum('bkgqs,bskd->bqkgd', p[..., span:], ctx_v)
        return o_win + o_ctx

    o = lax.map(block, (jnp.arange(nb), qb))
    return jnp.moveaxis(o, 0, 1).reshape(B, L, N_KV_HEADS, GQA, HEAD_DIM)


def _depthwise_conv(u, w, b):
    y = lax.conv_general_dilated(u, w[:, None, :], window_strides=(1,),
                                 padding=[(SSM_CONV // 2, SSM_CONV // 2)],
                                 dimension_numbers=('NWC', 'WIO', 'NWC'),
                                 feature_group_count=u.shape[-1])
    return y + b


def _ssd_chunked(x, dt, A, Bm, Cm, h0):
    b, L, H, P = x.shape
    G, N = Bm.shape[2], Bm.shape[3]
    J = H // G
    Q = SSM_CHUNK
    nc = L // Q
    a = (dt * A).reshape(b, nc, Q, G, J)
    xd = (x * dt[..., None]).reshape(b, nc, Q, G, J, P)
    Bc = Bm.reshape(b, nc, Q, G, N)
    Cc = Cm.reshape(b, nc, Q, G, N)
    a_cs = jnp.cumsum(a, axis=2)
    causal = jnp.tril(jnp.ones((Q, Q), bool))[:, :, None, None]
    seg = jnp.where(causal, a_cs[:, :, :, None] - a_cs[:, :, None, :], -jnp.inf)
    cb = jnp.einsum('bclgn,bcsgn->bclsg', Cc, Bc)
    y_diag = jnp.einsum('bclsg,bclsgj,bcsgjp->bclgjp', cb, jnp.exp(seg), xd)
    decay_end = jnp.exp(a_cs[:, :, -1:] - a_cs)
    states = jnp.einsum('bcsgn,bcsgj,bcsgjp->bcgjpn', Bc, decay_end, xd)
    chunk_decay = jnp.exp(a_cs[:, :, -1])

    def step(h, inp):
        st, dec = inp
        return h * dec[..., None, None] + st, h

    h_last, h_in = lax.scan(step, h0.reshape(b, G, J, P, N),
                            (jnp.moveaxis(states, 1, 0), jnp.moveaxis(chunk_decay, 1, 0)))
    h_in = jnp.moveaxis(h_in, 0, 1)
    y_off = jnp.einsum('bclgn,bcgjpn,bclgj->bclgjp', Cc, h_in, jnp.exp(a_cs))
    return (y_diag + y_off).reshape(b, L, H, P), h_last.reshape(b, H, P, N)


def _token_mix(h, w_in, w_out, pool_w, pool_scale, attn_sink, conv_w, conv_b, dt_bias, a_log, ssm_d,
               ssm_norm, ctx_k, ctx_v, ssm_h0):
    B, L, _ = h.shape
    cuts = np.cumsum(PROJ_SIZES)[:-1].tolist()
    u, q, k, v, z, xbc, dt_raw = jnp.split(h @ w_in, cuts, axis=-1)
    y_pool = _multiscale_pool(u, pool_w, pool_scale)
    q = q.reshape(B, L, N_KV_HEADS, GQA, HEAD_DIM)
    k = k.reshape(B, L, N_KV_HEADS, HEAD_DIM)
    v = v.reshape(B, L, N_KV_HEADS, HEAD_DIM)
    if ctx_k is None:
        o = _context_attention(q, k, v, attn_sink)
    else:
        cos, sin = _axial_rope(L)
        k = _apply_rope(k, cos, sin)
        o = _latent_attention(_apply_rope(q, cos, sin), k, v, ctx_k, ctx_v, attn_sink)
    xbc = jax.nn.silu(_depthwise_conv(xbc, conv_w, conv_b))
    xs, b_in, c_in = jnp.split(xbc, [SSM_INNER, SSM_INNER + SSM_GROUPS * SSM_STATE], axis=-1)
    xs = xs.reshape(B, L, SSM_HEADS, SSM_HEADDIM)
    b_in = b_in.reshape(B, L, SSM_GROUPS, SSM_STATE)
    c_in = c_in.reshape(B, L, SSM_GROUPS, SSM_STATE)
    dt = jax.nn.softplus(dt_raw.reshape(B, L, 2, SSM_HEADS) + dt_bias)
    A = -jnp.exp(a_log)
    if ssm_h0 is None:
        ssm_h0 = jnp.zeros((B, 2, SSM_HEADS, SSM_HEADDIM, SSM_STATE), f32)
    y_f, h_f = _ssd_chunked(xs, dt[:, :, 0], A[0], b_in, c_in, ssm_h0[:, 0])
    y_b, h_b = _ssd_chunked(jnp.flip(xs, 1), jnp.flip(dt[:, :, 1], 1), A[1], jnp.flip(b_in, 1),
                            jnp.flip(c_in, 1), ssm_h0[:, 1])
    y = y_f + jnp.flip(y_b, 1) + xs * ssm_d[:, None]
    y = y.reshape(B, L, SSM_INNER) * jax.nn.silu(z)
    y_ssm = _rms_norm(y, ssm_norm)
    y_mix = jnp.concatenate([y_pool, o.reshape(B, L, ATTN_WIDTH), y_ssm], axis=-1) @ w_out
    return y_mix, k, v, jnp.stack([h_f, h_b], axis=1)


def _mix_sublayer(x, mod, mix_w, pre1, post1, ctx_k, ctx_v, ssm_h0):
    shift1, scale1, gate1 = mod
    h = _rms_norm(x, pre1) * (1 + scale1) + shift1
    mix, k, v, st = _token_mix(h, *mix_w, ctx_k, ctx_v, ssm_h0)
    return x + gate1 * _rms_norm(mix, post1), k, v, st


def kernel(x_prompt, x_sample, cache_k, cache_v, state_ssm, c, c_ctx, w_ada, b_ada, norm_mix_pre,
           norm_mix_post, norm_ffn_pre, norm_ffn_post, w_in, w_out, pool_w, pool_scale, attn_sink,
           conv_w, conv_b, dt_bias, a_log, ssm_d, ssm_norm, ffn_w1, ffn_w3, ffn_w2, router_w, router_b,
           moe_w1, moe_w3, moe_w2):
    y_p, y_s = x_prompt, x_sample
    ks, vs, hs = [], [], []
    cond = jnp.concatenate([c_ctx[None, :], c], axis=0)
    for l in range(DEPTH):
        mix_w = (w_in[l], w_out[l], pool_w[l], pool_scale[l], attn_sink[l], conv_w[l], conv_b[l],
                 dt_bias[l], a_log[l], ssm_d[l], ssm_norm[l])
        mod = (jax.nn.silu(cond) @ w_ada[l] + b_ada[l])[:, None, :]
        mod_p = jnp.split(mod[:1], 6, axis=-1)
        mod_s = jnp.split(mod[1:], 6, axis=-1)
        y_p, k_l, v_l, h_l = _mix_sublayer(y_p, mod_p[:3], mix_w, norm_mix_pre[l], norm_mix_post[l],
                                           None, None, None)
        ks.append(k_l)
        vs.append(v_l)
        hs.append(h_l)
        y_s, _, _, _ = _mix_sublayer(y_s, mod_s[:3], mix_w, norm_mix_pre[l], norm_mix_post[l],
                                     cache_k[:, l], cache_v[:, l], state_ssm[:, l])
        h_p = _rms_norm(y_p, norm_ffn_pre[l]) * (1 + mod_p[4]) + mod_p[3]
        h_s = _rms_norm(y_s, norm_ffn_pre[l]) * (1 + mod_s[4]) + mod_s[3]
        h_all = jnp.concatenate([h_p.reshape(N_CTX_TOK, D_MODEL), h_s.reshape(N_LAT_TOK, D_MODEL)], axis=0)
        i = l // 2
        if l % 2 == 0:
            f_all = _ffn(h_all.astype(bf16), ffn_w1[i], ffn_w3[i], ffn_w2[i])
        else:
            f_all = _moe(h_all, router_w[i], router_b[i], moe_w1[i], moe_w3[i], moe_w2[i])
        f_p = f_all[:N_CTX_TOK].reshape(BATCH, SEQ, D_MODEL)
        f_s = f_all[N_CTX_TOK:].reshape(DEC_BATCH, DEC_SEQ, D_MODEL)
        y_p = y_p + mod_p[5] * _rms_norm(f_p, norm_ffn_post[l])
        y_s = y_s + mod_s[5] * _rms_norm(f_s, norm_ffn_post[l])
    new_cache_k = jnp.stack(ks, axis=1)
    new_cache_v = jnp.stack(vs, axis=1)
    new_state_ssm = jnp.stack(hs, axis=1)
    return (y_p, y_s, new_cache_k, new_cache_v, new_state_ssm)
```

```python
import functools

import numpy as np
import jax
import jax.numpy as jnp
from jax import lax
from jax.experimental import pallas as pl
from jax.experimental.pallas import tpu as pltpu

D_MODEL = 2048
BATCH = 16
SEQ = 256
DEPTH = 4
DEC_BATCH = 4
DEC_SEQ = 4096
PAST_LEN = 512
GRID_W = 64
POOL_WIDTH = D_MODEL // 4
ATTN_WIDTH = D_MODEL // 2
SSM_INNER = D_MODEL // 4
POOL_WINDOWS = (2, 4, 8, 16)
POOL_GROUPS = len(POOL_WINDOWS)
POOL_CH = POOL_WIDTH // POOL_GROUPS
HEAD_DIM = 128
N_HEADS = ATTN_WIDTH // HEAD_DIM
N_KV_HEADS = 2
GQA = N_HEADS // N_KV_HEADS
KV_WIDTH = N_KV_HEADS * HEAD_DIM
ATTN_WINDOW = 128
ATTN_BLOCK = 128
ATTN_SPAN = ATTN_BLOCK + 2 * ATTN_WINDOW
ATTN_SCALE = HEAD_DIM ** -0.5
ROPE_PAIRS = HEAD_DIM // 4
ROPE_BASE = 10000.0
SSM_HEADDIM = 64
SSM_HEADS = SSM_INNER // SSM_HEADDIM
SSM_GROUPS = 2
SSM_STATE = 128
SSM_CONV = 5
SSM_CHUNK = 128
CONV_DIM = SSM_INNER + 2 * SSM_GROUPS * SSM_STATE
D_FF = 7168
N_EXPERTS = 8
TOP_K = 2
RMS_EPS = 1e-6

N_CTX_TOK = BATCH * SEQ
N_LAT_TOK = DEC_BATCH * DEC_SEQ
N_TOK = N_CTX_TOK + N_LAT_TOK
N_COND = 1 + DEC_BATCH
COND_PAD = 8
assert N_CTX_TOK == DEC_SEQ

COL_U = 0
COL_Q = COL_U + POOL_WIDTH
COL_K = COL_Q + ATTN_WIDTH
COL_V = COL_K + KV_WIDTH
COL_Z = COL_V + KV_WIDTH
COL_XBC = COL_Z + SSM_INNER
PROJ_W = COL_XBC + CONV_DIM
DT_W = 2 * SSM_HEADS
LANE = 128
HALO = 8

V7X_VMEM_LIMIT = 56 * 1024 * 1024

FFN_TM = 1024
FFN_TF = 256
MOE_TM = 1024
PROJ_TM = 1024
PROJ_TN = 512
OUT_TM = 512
SEQ_TM = SEQ
ADA_TN = 1024

f32 = jnp.float32
bf16 = jnp.bfloat16


def _params(n_axes):
    return pltpu.CompilerParams(dimension_semantics=("arbitrary",) * n_axes,
                                vmem_limit_bytes=V7X_VMEM_LIMIT)


def _silu(x):
    return x * jax.nn.sigmoid(x)


def _rms(x):
    return x * lax.rsqrt(jnp.mean(x * x, axis=-1, keepdims=True) + RMS_EPS)


def _cond_row(tm):
    return lambda i: (i * tm) // DEC_SEQ


def _mod_spec(layer, tm, chunk):
    row = _cond_row(tm)
    return pl.BlockSpec((None, None, 1, D_MODEL), lambda i, *_: (layer, row(i), 0, chunk))


def _vec_spec(layer, width=None, col=0):
    return pl.BlockSpec((None, 1, width), lambda *_: (layer, 0, col))


def _ada_kernel(c_ref, w_ref, b_ref, o_ref):
    s = _silu(c_ref[...]).astype(bf16)
    o_ref[...] = jnp.dot(s, w_ref[...].astype(bf16), preferred_element_type=f32) + b_ref[...]


def _ada(cond, w_ada, b_ada):
    n = 6 * D_MODEL
    out = pl.pallas_call(
        _ada_kernel,
        grid=(DEPTH, n // ADA_TN),
        in_specs=[
            pl.BlockSpec((COND_PAD, D_MODEL), lambda l, j: (0, 0)),
            pl.BlockSpec((None, D_MODEL, ADA_TN), lambda l, j: (l, 0, j)),
            pl.BlockSpec((None, 1, ADA_TN), lambda l, j: (l, 0, j)),
        ],
        out_specs=pl.BlockSpec((None, COND_PAD, ADA_TN), lambda l, j: (l, 0, j)),
        out_shape=jax.ShapeDtypeStruct((DEPTH, COND_PAD, n), f32),
        compiler_params=_params(2),
        name="adaln_mod",
    )(cond, w_ada, b_ada.reshape(DEPTH, 1, n))
    return out.reshape(DEPTH, COND_PAD, 1, n)


def _inproj_kernel(x_ref, sh_ref, sc_ref, g_ref, w_ref, wdt_ref, o_ref, dt_ref, h_s):
    @pl.when(pl.program_id(1) == 0)
    def _():
        h = (_rms(x_ref[...]) * g_ref[...]) * (1.0 + sc_ref[...]) + sh_ref[...]
        hb = h.astype(bf16)
        h_s[...] = hb
        dt_ref[...] = jnp.dot(hb, wdt_ref[...], preferred_element_type=f32)

    o_ref[...] = jnp.dot(h_s[...], w_ref[...], preferred_element_type=f32)


def _inproj(x, mod, g_pre, w_main, w_dt, layer):
    tm, tn = PROJ_TM, PROJ_TN
    return pl.pallas_call(
        _inproj_kernel,
        grid=(N_TOK // tm, PROJ_W // tn),
        in_specs=[
            pl.BlockSpec((tm, D_MODEL), lambda i, j: (i, 0)),
            _mod_spec(layer, tm, 0),
            _mod_spec(layer, tm, 1),
            _vec_spec(layer, D_MODEL),
            pl.BlockSpec((None, D_MODEL, tn), lambda i, j: (layer, 0, j)),
            pl.BlockSpec((None, D_MODEL, LANE), lambda i, j: (layer, 0, 0)),
        ],
        out_specs=[
            pl.BlockSpec((tm, tn), lambda i, j: (i, j)),
            pl.BlockSpec((tm, LANE), lambda i, j: (i, 0)),
        ],
        out_shape=[jax.ShapeDtypeStruct((N_TOK, PROJ_W), f32),
                   jax.ShapeDtypeStruct((N_TOK, LANE), f32)],
        scratch_shapes=[pltpu.VMEM((tm, D_MODEL), bf16)],
        compiler_params=_params(2),
        name="in_proj",
    )(x, mod, mod, g_pre, w_main, w_dt)


def _seq_tile_position(i):
    is_ctx = i < N_CTX_TOK // SEQ_TM
    t0 = jnp.where(is_ctx, 0, ((i - N_CTX_TOK // SEQ_TM) * SEQ_TM) % DEC_SEQ)
    seq_len = jnp.where(is_ctx, SEQ, DEC_SEQ)
    return t0, seq_len


def _fill_extended(e_s, cur_ref, prev_ref, next_ref, t0, seq_len):
    first = t0 == 0
    last = t0 + SEQ_TM == seq_len
    e_s[0:HALO, :] = jnp.where(first, 0.0, prev_ref[...])
    e_s[HALO:HALO + SEQ_TM, :] = cur_ref[...]
    e_s[HALO + SEQ_TM:, :] = jnp.where(last, 0.0, next_ref[...])


def _halo_specs(width, col):
    per = SEQ_TM // HALO
    n8 = N_TOK // HALO
    return [
        pl.BlockSpec((SEQ_TM, width), lambda i, *a: (i, col(*a))),
        pl.BlockSpec((HALO, width), lambda i, *a: (jnp.maximum(i * per - 1, 0), col(*a))),
        pl.BlockSpec((HALO, width), lambda i, *a: (jnp.minimum((i + 1) * per, n8 - 1), col(*a))),
    ]


def _pool_kernel(cur_ref, prev_ref, next_ref, w_ref, sc_ref, o_ref, e_s):
    t0, seq_len = _seq_tile_position(pl.program_id(0))
    _fill_extended(e_s, cur_ref, prev_ref, next_ref, t0, seq_len)
    t = t0 + lax.broadcasted_iota(jnp.int32, (SEQ_TM, 1), 0)
    for g, w in enumerate(POOL_WINDOWS):
        lanes = pl.ds(g * POOL_CH, POOL_CH)
        acc = e_s[pl.ds(HALO - w // 2, SEQ_TM), lanes]
        for s in range(1 - w // 2, w // 2):
            acc = acc + e_s[pl.ds(HALO + s, SEQ_TM), lanes]
        cnt = jnp.minimum(t + w // 2, seq_len) - jnp.maximum(t - w // 2, 0)
        mean = acc / cnt.astype(f32)
        d = (mean - e_s[pl.ds(HALO, SEQ_TM), lanes]).astype(bf16)
        y = jnp.dot(d, w_ref[g].astype(bf16), preferred_element_type=f32) * sc_ref[:, g * POOL_CH:(g + 1) * POOL_CH]
        o_ref[:, g * POOL_CH:(g + 1) * POOL_CH] = y.astype(bf16)


def _pool(proj, pool_w, pool_scale, layer):
    return pl.pallas_call(
        _pool_kernel,
        grid=(N_TOK // SEQ_TM,),
        in_specs=_halo_specs(POOL_WIDTH, lambda: COL_U // POOL_WIDTH) + [
            pl.BlockSpec((None, POOL_GROUPS, POOL_CH, POOL_CH), lambda i: (layer, 0, 0, 0)),
            _vec_spec(layer, POOL_WIDTH),
        ],
        out_specs=pl.BlockSpec((SEQ_TM, POOL_WIDTH), lambda i: (i, 0)),
        out_shape=jax.ShapeDtypeStruct((N_TOK, POOL_WIDTH), bf16),
        scratch_shapes=[pltpu.VMEM((SEQ_TM + 2 * HALO, POOL_WIDTH), f32)],
        compiler_params=_params(1),
        name="pool_mixer",
    )(proj, proj, proj, pool_w, pool_scale)


CONV_TN = 512


def _conv_kernel(cur_ref, prev_ref, next_ref, w_ref, b_ref, o_ref, e_s):
    t0, seq_len = _seq_tile_position(pl.program_id(0))
    _fill_extended(e_s, cur_ref, prev_ref, next_ref, t0, seq_len)
    acc = b_ref[...] + e_s[pl.ds(HALO - SSM_CONV // 2, SEQ_TM), :] * w_ref[0:1, :]
    for k in range(1, SSM_CONV):
        acc = acc + e_s[pl.ds(HALO + k - SSM_CONV // 2, SEQ_TM), :] * w_ref[k:k + 1, :]
    o_ref[...] = _silu(acc)


def _conv(proj, conv_w, conv_b, layer):
    col0 = COL_XBC // CONV_TN
    return pl.pallas_call(
        _conv_kernel,
        grid=(N_TOK // SEQ_TM, CONV_DIM // CONV_TN),
        in_specs=_halo_specs(CONV_TN, lambda j: col0 + j) + [
            pl.BlockSpec((None, SSM_CONV, CONV_TN), lambda i, j: (layer, 0, j)),
            pl.BlockSpec((None, 1, CONV_TN), lambda i, j: (layer, 0, j)),
        ],
        out_specs=pl.BlockSpec((SEQ_TM, CONV_TN), lambda i, j: (i, j)),
        out_shape=jax.ShapeDtypeStruct((N_TOK, CONV_DIM), f32),
        scratch_shapes=[pltpu.VMEM((SEQ_TM + 2 * HALO, CONV_TN), f32)],
        compiler_params=_params(2),
        name="ssm_conv",
    )(proj, proj, proj, conv_w, conv_b)


def _softmax_sink(s, sink):
    m = jnp.maximum(jnp.max(s, axis=-1, keepdims=True), sink)
    e = jnp.exp(s - m)
    return e / (jnp.sum(e, axis=-1, keepdims=True) + jnp.exp(sink - m))


def _qk(q, k):
    return lax.dot_general(q, k, (((1,), (1,)), ((), ())), preferred_element_type=f32)


def _ctx_attn_kernel(sink_ref, q_ref, k_ref, v_ref, o_ref):
    kvh = pl.program_id(1)
    k = k_ref[...].astype(bf16)
    v = v_ref[...].astype(bf16)
    for g in range(GQA):
        q = q_ref[:, g * HEAD_DIM:(g + 1) * HEAD_DIM].astype(bf16)
        p = _softmax_sink(_qk(q, k) * ATTN_SCALE, sink_ref[kvh * GQA + g])
        o = jnp.dot(p.astype(bf16), v, preferred_element_type=f32)
        o_ref[:, g * HEAD_DIM:(g + 1) * HEAD_DIM] = o.astype(bf16)


def _ctx_attention(proj, sink):
    qw = GQA * HEAD_DIM
    return pl.pallas_call(
        _ctx_attn_kernel,
        grid=(BATCH, N_KV_HEADS),
        in_specs=[
            pl.BlockSpec(memory_space=pltpu.SMEM),
            pl.BlockSpec((SEQ, qw), lambda b, h: (b, COL_Q // qw + h)),
            pl.BlockSpec((SEQ, HEAD_DIM), lambda b, h: (b, COL_K // HEAD_DIM + h)),
            pl.BlockSpec((SEQ, HEAD_DIM), lambda b, h: (b, COL_V // HEAD_DIM + h)),
        ],
        out_specs=pl.BlockSpec((SEQ, qw), lambda b, h: (b, h)),
        out_shape=jax.ShapeDtypeStruct((N_CTX_TOK, ATTN_WIDTH), bf16),
        compiler_params=_params(2),
        name="ctx_attention",
    )(sink, proj, proj, proj)


def _rope_tables():
    t = np.arange(DEC_SEQ)
    inv = ROPE_BASE ** (-np.arange(ROPE_PAIRS, dtype=np.float64) / ROPE_PAIRS)
    ang_r = (t // GRID_W)[:, None] * inv
    ang_c = (t % GRID_W)[:, None] * inv
    cos = np.concatenate([np.cos(ang_r), np.cos(ang_r), np.cos(ang_c), np.cos(ang_c)], axis=1)
    sin = np.concatenate([-np.sin(ang_r), np.sin(ang_r), -np.sin(ang_c), np.sin(ang_c)], axis=1)
    return jnp.asarray(cos, f32), jnp.asarray(sin, f32)


def _rope(x, cos, sin):
    lane = lax.broadcasted_iota(jnp.int32, x.shape, 1)
    partner = jnp.where(lane % (2 * ROPE_PAIRS) < ROPE_PAIRS,
                        pltpu.roll(x, HEAD_DIM - ROPE_PAIRS, 1), pltpu.roll(x, ROPE_PAIRS, 1))
    return x * cos + partner * sin


ROPE_ROWS = 512


def _lat_attn_kernel(sink_ref, q0_ref, q1_ref, k_ref, v_ref, ck_ref, cv_ref, cosq_ref, sinq_ref,
                     cos_ref, sin_ref, o_ref, k_s, v_s, ck_s, cv_s):
    qb = pl.program_id(1)

    @pl.when(qb == 0)
    def _():
        zeros = jnp.zeros((ATTN_WINDOW, KV_WIDTH), bf16)
        k_s[0:ATTN_WINDOW, :] = zeros
        v_s[0:ATTN_WINDOW, :] = zeros
        k_s[ATTN_WINDOW + DEC_SEQ:, :] = zeros
        v_s[ATTN_WINDOW + DEC_SEQ:, :] = zeros

        def stage(c, carry):
            r = pl.multiple_of(c * ROPE_ROWS, ROPE_ROWS)
            cos = cos_ref[pl.ds(r, ROPE_ROWS), :]
            sin = sin_ref[pl.ds(r, ROPE_ROWS), :]
            for h in range(N_KV_HEADS):
                kh = k_ref[pl.ds(r, ROPE_ROWS), h * HEAD_DIM:(h + 1) * HEAD_DIM]
                k_s[pl.ds(ATTN_WINDOW + r, ROPE_ROWS), h * HEAD_DIM:(h + 1) * HEAD_DIM] = _rope(kh, cos, sin).astype(bf16)
            v_s[pl.ds(ATTN_WINDOW + r, ROPE_ROWS), :] = v_ref[pl.ds(r, ROPE_ROWS), :].astype(bf16)
            return carry

        lax.fori_loop(0, DEC_SEQ // ROPE_ROWS, stage, 0)
        ck_s[...] = ck_ref[...].astype(bf16)
        cv_s[...] = cv_ref[...].astype(bf16)

    start = pl.multiple_of(qb * ATTN_BLOCK, ATTN_BLOCK)
    rows = GQA * ATTN_BLOCK
    a = lax.broadcasted_iota(jnp.int32, (rows, ATTN_SPAN), 0) % ATTN_BLOCK
    b = lax.broadcasted_iota(jnp.int32, (rows, ATTN_SPAN), 1)
    pos = start - ATTN_WINDOW + b
    mask = (b >= a) & (b <= a + 2 * ATTN_WINDOW) & (pos >= 0) & (pos < DEC_SEQ)
    head_of_row = lax.broadcasted_iota(jnp.int32, (rows, 1), 0) // ATTN_BLOCK
    cos = cosq_ref[...]
    sin = sinq_ref[...]
    for h, q_ref in enumerate((q0_ref, q1_ref)):
        q = jnp.concatenate(
            [_rope(q_ref[:, g * HEAD_DIM:(g + 1) * HEAD_DIM], cos, sin) for g in range(GQA)], axis=0).astype(bf16)
        sink = jnp.zeros((rows, 1), f32)
        for g in range(GQA):
            sink = jnp.where(head_of_row == g, sink_ref[h * GQA + g], sink)
        lanes = pl.ds(h * HEAD_DIM, HEAD_DIM)
        kw = k_s[pl.ds(start, ATTN_SPAN), lanes]
        vw = v_s[pl.ds(start, ATTN_SPAN), lanes]
        s_win = jnp.where(mask, _qk(q, kw) * ATTN_SCALE, -jnp.inf)
        s_ctx = _qk(q, ck_s[:, lanes]) * ATTN_SCALE
        m = jnp.maximum(jnp.maximum(jnp.max(s_win, axis=-1, keepdims=True),
                                    jnp.max(s_ctx, axis=-1, keepdims=True)), sink)
        e_win = jnp.exp(s_win - m)
        e_ctx = jnp.exp(s_ctx - m)
        den = (jnp.sum(e_win, axis=-1, keepdims=True) + jnp.sum(e_ctx, axis=-1, keepdims=True)
               + jnp.exp(sink - m))
        o = (jnp.dot((e_win / den).astype(bf16), vw, preferred_element_type=f32)
             + jnp.dot((e_ctx / den).astype(bf16), cv_s[:, lanes], preferred_element_type=f32))
        for g in range(GQA):
            col = (h * GQA + g) * HEAD_DIM
            o_ref[:, col:col + HEAD_DIM] = o[g * ATTN_BLOCK:(g + 1) * ATTN_BLOCK].astype(bf16)


def _lat_attention(proj, cache_k, cache_v, sink, layer):
    cos, sin = _rope_tables()
    qw = GQA * HEAD_DIM
    nqb = DEC_SEQ // ATTN_BLOCK
    row0 = N_CTX_TOK // ATTN_BLOCK
    seq0 = N_CTX_TOK // DEC_SEQ
    ck = cache_k.reshape(DEC_BATCH, DEPTH, PAST_LEN, KV_WIDTH)
    cv = cache_v.reshape(DEC_BATCH, DEPTH, PAST_LEN, KV_WIDTH)
    ctx_spec = pl.BlockSpec((None, None, PAST_LEN, KV_WIDTH), lambda b, n: (b, layer, 0, 0))
    return pl.pallas_call(
        _lat_attn_kernel,
        grid=(DEC_BATCH, nqb),
        in_specs=[
            pl.BlockSpec(memory_space=pltpu.SMEM),
            pl.BlockSpec((ATTN_BLOCK, qw), lambda b, n: (row0 + b * nqb + n, COL_Q // qw)),
            pl.BlockSpec((ATTN_BLOCK, qw), lambda b, n: (row0 + b * nqb + n, COL_Q // qw + 1)),
            pl.BlockSpec((DEC_SEQ, KV_WIDTH), lambda b, n: (seq0 + b, COL_K // KV_WIDTH)),
            pl.BlockSpec((DEC_SEQ, KV_WIDTH), lambda b, n: (seq0 + b, COL_V // KV_WIDTH)),
            ctx_spec, ctx_spec,
            pl.BlockSpec((ATTN_BLOCK, HEAD_DIM), lambda b, n: (n, 0)),
            pl.BlockSpec((ATTN_BLOCK, HEAD_DIM), lambda b, n: (n, 0)),
            pl.BlockSpec((DEC_SEQ, HEAD_DIM), lambda b, n: (0, 0)),
            pl.BlockSpec((DEC_SEQ, HEAD_DIM), lambda b, n: (0, 0)),
        ],
        out_specs=pl.BlockSpec((ATTN_BLOCK, ATTN_WIDTH), lambda b, n: (b * nqb + n, 0)),
        out_shape=jax.ShapeDtypeStruct((N_LAT_TOK, ATTN_WIDTH), bf16),
        scratch_shapes=[
            pltpu.VMEM((DEC_SEQ + 2 * ATTN_WINDOW, KV_WIDTH), bf16),
            pltpu.VMEM((DEC_SEQ + 2 * ATTN_WINDOW, KV_WIDTH), bf16),
            pltpu.VMEM((PAST_LEN, KV_WIDTH), bf16),
            pltpu.VMEM((PAST_LEN, KV_WIDTH), bf16),
        ],
        compiler_params=_params(2),
        name="latent_attention",
    )(sink, proj, proj, proj, proj, ck, cv, cos, sin, cos, sin)


SSD_XW = SSM_HEADS * SSM_HEADDIM
SSD_LW = SSM_HEADS * LANE
PAIR_W = 2 * SSM_HEADDIM
GROUP_W = SSD_XW // SSM_GROUPS


def _split3(x):
    hi = x.astype(bf16)
    r = x - hi.astype(f32)
    mid = r.astype(bf16)
    lo = (r - mid.astype(f32)).astype(bf16)
    return hi, mid, lo


def _dot_exact_rhs(sel, x):
    return sum(jnp.dot(sel, p, preferred_element_type=f32) for p in _split3(x))


def _dot_exact_lhs(x, sel):
    return sum(jnp.dot(p, sel, preferred_element_type=f32) for p in _split3(x))


def _ssd_constants():
    q = SSM_CHUNK
    lower = np.tril(np.ones((q, q), np.float32))
    tri = np.stack([lower, lower.T])
    x_wide = np.zeros((2, LANE, SSD_LW), np.float32)
    x_head = np.zeros((2, LANE, SSD_XW), np.float32)
    for d in range(2):
        for j in range(SSM_HEADS):
            x_wide[d, d * SSM_HEADS + j, j * LANE:(j + 1) * LANE] = 1.0
            x_head[d, d * SSM_HEADS + j, j * SSM_HEADDIM:(j + 1) * SSM_HEADDIM] = 1.0
    return jnp.asarray(tri, bf16), jnp.asarray(x_wide, bf16), jnp.asarray(x_head, bf16)


def _ssd_direction(d, act, dt_raw, h_s, bias, a_neg, tri_ref, xw_ref, xh_ref):
    q = SSM_CHUNK
    dt = jnp.logaddexp(dt_raw + bias, 0.0)
    a = dt * a_neg
    cs = _dot_exact_rhs(tri_ref[d], a)
    cs_wide = _dot_exact_lhs(cs, xw_ref[d])
    cs_head = _dot_exact_lhs(cs, xh_ref[d])
    dt_head = _dot_exact_lhs(dt, xh_ref[d])
    cs_t = cs.T
    edge = cs_head[q - 1:q, :] if d == 0 else cs_head[0:1, :]
    xd = act[:, 0:SSD_XW] * dt_head
    xdb = xd.astype(bf16)
    xdw = (xd * jnp.exp(edge - cs_head)).astype(bf16)
    li = lax.broadcasted_iota(jnp.int32, (q, q), 0)
    si = lax.broadcasted_iota(jnp.int32, (q, q), 1)
    seen = (li >= si) if d == 0 else (li <= si)
    lane = lax.broadcasted_iota(jnp.int32, (q, PAIR_W), 1)
    h_in = h_s[...]
    ys = []
    for g in range(SSM_GROUPS):
        bm = act[:, SSD_XW + g * SSM_STATE:SSD_XW + (g + 1) * SSM_STATE]
        cm = act[:, SSD_XW + (SSM_GROUPS + g) * SSM_STATE:SSD_XW + (SSM_GROUPS + g + 1) * SSM_STATE]
        cmb = cm.astype(bf16)
        cb = _qk(cmb, bm.astype(bf16))
        cols = slice(g * GROUP_W, (g + 1) * GROUP_W)
        y_off = jnp.dot(cmb, h_in[:, cols].astype(bf16), preferred_element_type=f32) * jnp.exp(cs_head[:, cols])
        y_diag = []
        for pr in range(GROUP_W // PAIR_W):
            pcols = slice(g * GROUP_W + pr * PAIR_W, g * GROUP_W + (pr + 1) * PAIR_W)
            halves = []
            for jj in range(2):
                j = (g * GROUP_W + pr * PAIR_W) // SSM_HEADDIM + jj
                row = d * SSM_HEADS + j
                seg = cs_wide[:, j * LANE:(j + 1) * LANE] - cs_t[row:row + 1, :]
                m = (cb * jnp.exp(jnp.where(seen, seg, -jnp.inf))).astype(bf16)
                halves.append(jnp.dot(m, xdb[:, pcols], preferred_element_type=f32))
            y_diag.append(jnp.where(lane < SSM_HEADDIM, halves[0], halves[1]))
        ys.append(jnp.concatenate(y_diag, axis=1) + y_off)
        st = jnp.dot(bm.T.astype(bf16), xdw[:, cols], preferred_element_type=f32)
        h_s[:, cols] = h_in[:, cols] * jnp.exp(edge[:, cols]) + st
    return jnp.concatenate(ys, axis=1)


def _ssd_kernel(*refs, has_h0, emit_state):
    refs = list(refs)
    af_ref, ab_ref, dtf_ref, dtb_ref = refs[:4]
    refs = refs[4:]
    h0_ref = refs.pop(0) if has_h0 else None
    bias_ref, alog_ref, tri_ref, xw_ref, xh_ref = refs[:5]
    refs = refs[5:]
    yf_ref, yb_ref = refs[:2]
    refs = refs[2:]
    hout_ref = refs.pop(0) if emit_state else None
    hf_s, hb_s = refs
    c = pl.program_id(1)
    blocks = SSD_XW // LANE

    @pl.when(c == 0)
    def _():
        for d, h_s in enumerate((hf_s, hb_s)):
            if has_h0:
                for k in range(blocks):
                    h_s[:, k * LANE:(k + 1) * LANE] = h0_ref[d, k * LANE:(k + 1) * LANE, :].T
            else:
                h_s[...] = jnp.zeros_like(h_s)

    lane = lax.broadcasted_iota(jnp.int32, (1, LANE), 1)
    a_neg = jnp.where(lane < DT_W, -jnp.exp(alog_ref[...]), 0.0)
    bias = bias_ref[...]
    yf_ref[...] = _ssd_direction(0, af_ref[...], dtf_ref[...], hf_s, bias, a_neg, tri_ref, xw_ref, xh_ref)
    yb_ref[...] = _ssd_direction(1, ab_ref[...], dtb_ref[...], hb_s, bias, a_neg, tri_ref, xw_ref, xh_ref)

    if emit_state:
        @pl.when(c == pl.num_programs(1) - 1)
        def _():
            for d, h_s in enumerate((hf_s, hb_s)):
                for k in range(blocks):
                    hout_ref[d, k * LANE:(k + 1) * LANE, :] = h_s[:, k * LANE:(k + 1) * LANE].T


def _ssd(act, dt_raw, h0, dt_bias, a_log, layer, *, n_seq, seq_len, row0, emit_state):
    q = SSM_CHUNK
    nc = seq_len // q
    blk0 = row0 // q
    tri, x_wide, x_head = _ssd_constants()
    fwd = lambda s, c: (blk0 + s * nc + c, 0)
    bwd = lambda s, c: (blk0 + s * nc + nc - 1 - c, 0)
    const3 = lambda s, c: (0, 0, 0)
    in_specs = [
        pl.BlockSpec((q, CONV_DIM), fwd), pl.BlockSpec((q, CONV_DIM), bwd),
        pl.BlockSpec((q, LANE), fwd), pl.BlockSpec((q, LANE), bwd),
    ]
    args = [act, act, dt_raw, dt_raw]
    if h0 is not None:
        in_specs.append(pl.BlockSpec((None, None, 2, SSD_XW, SSM_STATE), lambda s, c: (s, layer, 0, 0, 0)))
        args.append(h0)
    in_specs += [
        _vec_spec(layer, LANE), _vec_spec(layer, LANE),
        pl.BlockSpec(tri.shape, const3), pl.BlockSpec(x_wide.shape, const3), pl.BlockSpec(x_head.shape, const3),
    ]
    args += [dt_bias, a_log, tri, x_wide, x_head]
    rows = n_seq * seq_len
    out_specs = [pl.BlockSpec((q, SSD_XW), lambda s, c: (s * nc + c, 0)),
                 pl.BlockSpec((q, SSD_XW), lambda s, c: (s * nc + nc - 1 - c, 0))]
    out_shape = [jax.ShapeDtypeStruct((rows, SSD_XW), f32)] * 2
    if emit_state:
        out_specs.append(pl.BlockSpec((None, 2, SSD_XW, SSM_STATE), lambda s, c: (s, 0, 0, 0)))
        out_shape.append(jax.ShapeDtypeStruct((n_seq, 2, SSD_XW, SSM_STATE), f32))
    return pl.pallas_call(
        functools.partial(_ssd_kernel, has_h0=h0 is not None, emit_state=emit_state),
        grid=(n_seq, nc),
        in_specs=in_specs,
        out_specs=out_specs,
        out_shape=out_shape,
        scratch_shapes=[pltpu.VMEM((SSM_STATE, SSD_XW), f32), pltpu.VMEM((SSM_STATE, SSD_XW), f32)],
        compiler_params=_params(2),
        name="ssd_scan",
    )(*args)


OUT_TN = 512
OUT_NT = D_MODEL // OUT_TN


def _outproj_kernel(yp_ref, o_ref, yf_ref, yb_ref, xs_ref, z_ref, x_ref, gate_ref, sh2_ref, sc2_ref,
                    dx_ref, gssm_ref, gpost_ref, gpre2_ref, w_ref, x1_ref, h2_ref, a_s, m_s):
    j = pl.program_id(1)

    @pl.when(j == 0)
    def _():
        y = (yf_ref[...] + yb_ref[...] + xs_ref[...] * dx_ref[...]) * _silu(z_ref[...])
        a_s[:, 0:POOL_WIDTH] = yp_ref[...]
        a_s[:, POOL_WIDTH:POOL_WIDTH + ATTN_WIDTH] = o_ref[...]
        a_s[:, POOL_WIDTH + ATTN_WIDTH:] = (_rms(y) * gssm_ref[...]).astype(bf16)

    m_s[j] = jnp.dot(a_s[...], w_ref[...], preferred_element_type=f32)

    @pl.when(j == OUT_NT - 1)
    def _():
        ss = sum(jnp.sum(m_s[k] * m_s[k], axis=-1, keepdims=True) for k in range(OUT_NT))
        r = lax.rsqrt(ss / D_MODEL + RMS_EPS)
        ss1 = jnp.zeros_like(ss)
        for k in range(OUT_NT):
            c = slice(k * OUT_TN, (k + 1) * OUT_TN)
            x1 = x_ref[:, c] + gate_ref[:, c] * ((m_s[k] * r) * gpost_ref[:, c])
            x1_ref[:, c] = x1
            ss1 = ss1 + jnp.sum(x1 * x1, axis=-1, keepdims=True)
        r1 = lax.rsqrt(ss1 / D_MODEL + RMS_EPS)
        for k in range(OUT_NT):
            c = slice(k * OUT_TN, (k + 1) * OUT_TN)
            h2 = ((x1_ref[:, c] * r1) * gpre2_ref[:, c]) * (1.0 + sc2_ref[:, c]) + sh2_ref[:, c]
            h2_ref[:, c] = h2.astype(bf16)


def _outproj(y_pool, o_attn, y_f, y_b, act, proj, x, mod, d_skip, g_ssm, g_post, g_pre2, w_out, layer):
    tm = OUT_TM
    row = lambda i, j: (i, 0)
    return pl.pallas_call(
        _outproj_kernel,
        grid=(N_TOK // tm, OUT_NT),
        in_specs=[
            pl.BlockSpec((tm, POOL_WIDTH), row),
            pl.BlockSpec((tm, ATTN_WIDTH), row),
            pl.BlockSpec((tm, SSM_INNER), row),
            pl.BlockSpec((tm, SSM_INNER), row),
            pl.BlockSpec((tm, SSM_INNER), row),
            pl.BlockSpec((tm, SSM_INNER), lambda i, j: (i, COL_Z // SSM_INNER)),
            pl.BlockSpec((tm, D_MODEL), row),
            _mod_spec(layer, tm, 2), _mod_spec(layer, tm, 3), _mod_spec(layer, tm, 4),
            _vec_spec(layer, SSM_INNER), _vec_spec(layer, SSM_INNER),
            _vec_spec(layer, D_MODEL), _vec_spec(layer, D_MODEL),
            pl.BlockSpec((None, D_MODEL, OUT_TN), lambda i, j: (layer, 0, j)),
        ],
        out_specs=[pl.BlockSpec((tm, D_MODEL), row), pl.BlockSpec((tm, D_MODEL), row)],
        out_shape=[jax.ShapeDtypeStruct((N_TOK, D_MODEL), f32), jax.ShapeDtypeStruct((N_TOK, D_MODEL), bf16)],
        scratch_shapes=[pltpu.VMEM((tm, D_MODEL), bf16), pltpu.VMEM((OUT_NT, tm, OUT_TN), f32)],
        compiler_params=_params(2),
        name="out_proj",
    )(y_pool, o_attn, y_f, y_b, act, proj, x, mod, mod, mod, d_skip, g_ssm, g_post, g_pre2, w_out)


def _swiglu_step(h, w1_ref, w3_ref, w2_ref):
    a = jnp.dot(h, w1_ref[...].astype(bf16), preferred_element_type=f32)
    b = jnp.dot(h, w3_ref[...].astype(bf16), preferred_element_type=f32)
    g = (_silu(a) * b).astype(bf16)
    return jnp.dot(g, w2_ref[...].astype(bf16), preferred_element_type=f32)


def _ffn_kernel(h_ref, w1_ref, w3_ref, w2_ref, o_ref):
    j = pl.program_id(1)
    p = _swiglu_step(h_ref[...], w1_ref, w3_ref, w2_ref)

    @pl.when(j == 0)
    def _():
        o_ref[...] = p

    @pl.when(j > 0)
    def _():
        o_ref[...] += p


def _ffn(h, w1, w3, w2):
    T, D = h.shape
    F = w1.shape[1]
    return pl.pallas_call(
        _ffn_kernel,
        grid=(T // FFN_TM, F // FFN_TF),
        in_specs=[
            pl.BlockSpec((FFN_TM, D), lambda i, j: (i, 0)),
            pl.BlockSpec((D, FFN_TF), lambda i, j: (0, j)),
            pl.BlockSpec((D, FFN_TF), lambda i, j: (0, j)),
            pl.BlockSpec((FFN_TF, D), lambda i, j: (j, 0)),
        ],
        out_specs=pl.BlockSpec((FFN_TM, D), lambda i, j: (i, 0)),
        out_shape=jax.ShapeDtypeStruct((T, D), f32),
        compiler_params=_params(2),
        name="ffn_swiglu",
    )(h, w1, w3, w2)


def _moe_kernel(be_ref, nv_ref, x_ref, w1_ref, w3_ref, w2_ref, o_ref):
    i = pl.program_id(0)
    j = pl.program_id(1)

    @pl.when(i < nv_ref[0])
    def _():
        p = _swiglu_step(x_ref[...], w1_ref, w3_ref, w2_ref)

        @pl.when(j == 0)
        def _():
            o_ref[...] = p

        @pl.when(j > 0)
        def _():
            o_ref[...] += p

    @pl.when(jnp.logical_and(i >= nv_ref[0], j == 0))
    def _():
        o_ref[...] = jnp.zeros_like(o_ref)


def _moe_experts(xb, block_e, n_valid, w1, w3, w2):
    R, D = xb.shape
    F = w1.shape[2]
    nf = F // FFN_TF
    nb = R // MOE_TM

    def jeff(i, j, nv):
        return jnp.where(i < nv[0], j, nf - 1)

    return pl.pallas_call(
        _moe_kernel,
        grid_spec=pltpu.PrefetchScalarGridSpec(
            num_scalar_prefetch=2,
            grid=(nb, nf),
            in_specs=[
                pl.BlockSpec((MOE_TM, D), lambda i, j, be, nv: (i, 0)),
                pl.BlockSpec((None, D, FFN_TF), lambda i, j, be, nv: (be[i], 0, jeff(i, j, nv))),
                pl.BlockSpec((None, D, FFN_TF), lambda i, j, be, nv: (be[i], 0, jeff(i, j, nv))),
                pl.BlockSpec((None, FFN_TF, D), lambda i, j, be, nv: (be[i], jeff(i, j, nv), 0)),
            ],
            out_specs=pl.BlockSpec((MOE_TM, D), lambda i, j, be, nv: (i, 0)),
        ),
        out_shape=jax.ShapeDtypeStruct((R, D), f32),
        compiler_params=_params(2),
        name="moe_swiglu",
    )(block_e, n_valid, xb, w1, w3, w2)


def _moe(h, router_w, router_b, w1, w3, w2):
    T, D = h.shape
    TK = T * TOP_K
    logits = jnp.dot(h, router_w.astype(bf16), preferred_element_type=f32) + router_b
    top_logit, top_e = lax.top_k(logits, TOP_K)
    gate = jax.nn.softmax(top_logit, axis=-1)
    flat_e = top_e.reshape(-1)
    flat_tok = jnp.arange(TK, dtype=jnp.int32) // TOP_K
    flat_gate = gate.reshape(-1)
    order = jnp.argsort(flat_e)
    sorted_e = flat_e[order]
    counts = jnp.bincount(flat_e, length=N_EXPERTS)
    starts = jnp.cumsum(counts) - counts
    padded = (counts + MOE_TM - 1) // MOE_TM * MOE_TM
    pad_ends = jnp.cumsum(padded)
    pad_starts = pad_ends - padded
    dest = pad_starts[sorted_e] + jnp.arange(TK) - starts[sorted_e]
    n_blocks = TK // MOE_TM + N_EXPERTS
    rows = n_blocks * MOE_TM
    row_tok = jnp.zeros((rows,), jnp.int32).at[dest].set(flat_tok[order])
    row_gate = jnp.zeros((rows,), f32).at[dest].set(flat_gate[order])
    n_valid = (pad_ends[-1] // MOE_TM).astype(jnp.int32)
    block_e = jnp.searchsorted(pad_ends, jnp.arange(n_blocks) * MOE_TM, side='right')
    block_e = jnp.minimum(block_e, N_EXPERTS - 1).astype(jnp.int32)
    last_e = block_e[jnp.maximum(n_valid - 1, 0)]
    block_e = jnp.where(jnp.arange(n_blocks) < n_valid, block_e, last_e)
    xb = h[row_tok]
    out = _moe_experts(xb, block_e, n_valid.reshape(1), w1, w3, w2)
    out = out * row_gate[:, None]
    return jnp.zeros((T, D), f32).at[row_tok].add(out)


RES_TM = 512


def _resid_kernel(x_ref, f_ref, gate_ref, g_ref, o_ref):
    o_ref[...] = x_ref[...] + gate_ref[...] * (_rms(f_ref[...]) * g_ref[...])


def _resid(x, f, mod, g_post, layer):
    tm = RES_TM
    row = lambda i: (i, 0)
    return pl.pallas_call(
        _resid_kernel,
        grid=(N_TOK // tm,),
        in_specs=[pl.BlockSpec((tm, D_MODEL), row), pl.BlockSpec((tm, D_MODEL), row),
                  _mod_spec(layer, tm, 5), _vec_spec(layer, D_MODEL)],
        out_specs=pl.BlockSpec((tm, D_MODEL), row),
        out_shape=jax.ShapeDtypeStruct((N_TOK, D_MODEL), f32),
        compiler_params=_params(1),
        name="ffn_residual",
    )(x, f, mod, g_post)


def _row_params(p, width=None):
    p = p.reshape(DEPTH, -1)
    if width is not None and p.shape[1] < width:
        p = jnp.pad(p, ((0, 0), (0, width - p.shape[1])))
    return p[:, None, :]


def _mix_layer(x, mod, layer, prm, cache_k, cache_v, state_ssm):
    proj, dt_raw = _inproj(x, mod, prm['g_pre1'], prm['w_in'], prm['w_dt'], layer)
    y_pool = _pool(proj, prm['pool_w'], prm['pool_scale'], layer)
    act = _conv(proj, prm['conv_w'], prm['conv_b'], layer)
    sink = prm['attn_sink'][layer]
    o_ctx = _ctx_attention(proj, sink)
    o_lat = _lat_attention(proj, cache_k, cache_v, sink, layer)
    o_attn = jnp.concatenate([o_ctx, o_lat], axis=0)
    yf_c, yb_c, h_ctx = _ssd(act, dt_raw, None, prm['dt_bias'], prm['a_log'], layer,
                             n_seq=BATCH, seq_len=SEQ, row0=0, emit_state=True)
    yf_l, yb_l = _ssd(act, dt_raw, state_ssm, prm['dt_bias'], prm['a_log'], layer,
                      n_seq=DEC_BATCH, seq_len=DEC_SEQ, row0=N_CTX_TOK, emit_state=False)
    y_f = jnp.concatenate([yf_c, yf_l], axis=0)
    y_b = jnp.concatenate([yb_c, yb_l], axis=0)
    x1, h2 = _outproj(y_pool, o_attn, y_f, y_b, act, proj, x, mod, prm['d_skip'], prm['g_ssm'],
                      prm['g_post1'], prm['g_pre2'], prm['w_out'], layer)
    k_ctx = proj[:N_CTX_TOK, COL_K:COL_K + KV_WIDTH].reshape(BATCH, SEQ, N_KV_HEADS, HEAD_DIM)
    v_ctx = proj[:N_CTX_TOK, COL_V:COL_V + KV_WIDTH].reshape(BATCH, SEQ, N_KV_HEADS, HEAD_DIM)
    h_ctx = h_ctx.reshape(BATCH, 2, SSM_HEADS, SSM_HEADDIM, SSM_STATE)
    return x1, h2, k_ctx, v_ctx, h_ctx


def _prepare(norm_mix_pre, norm_mix_post, norm_ffn_pre, norm_ffn_post, w_in, w_out, pool_w, pool_scale,
             attn_sink, conv_w, conv_b, dt_bias, a_log, ssm_d, ssm_norm):
    w_dt = jnp.pad(w_in[:, :, PROJ_W:], ((0, 0), (0, 0), (0, LANE - DT_W)))
    return dict(
        g_pre1=_row_params(norm_mix_pre), g_post1=_row_params(norm_mix_post),
        g_pre2=_row_params(norm_ffn_pre), g_post2=_row_params(norm_ffn_post),
        w_in=w_in[:, :, :PROJ_W].astype(bf16), w_dt=w_dt.astype(bf16), w_out=w_out.astype(bf16),
        pool_w=pool_w, pool_scale=_row_params(pool_scale), attn_sink=attn_sink,
        conv_w=conv_w, conv_b=_row_params(conv_b),
        dt_bias=_row_params(dt_bias, LANE), a_log=_row_params(a_log, LANE),
        d_skip=_row_params(jnp.repeat(ssm_d, SSM_HEADDIM, axis=1)), g_ssm=_row_params(ssm_norm),
    )


def kernel(x_prompt, x_sample, cache_k, cache_v, state_ssm, c, c_ctx, w_ada, b_ada, norm_mix_pre,
           norm_mix_post, norm_ffn_pre, norm_ffn_post, w_in, w_out, pool_w, pool_scale, attn_sink,
           conv_w, conv_b, dt_bias, a_log, ssm_d, ssm_norm, ffn_w1, ffn_w3, ffn_w2, router_w, router_b,
           moe_w1, moe_w3, moe_w2):
    prm = _prepare(norm_mix_pre, norm_mix_post, norm_ffn_pre, norm_ffn_post, w_in, w_out, pool_w, pool_scale,
                   attn_sink, conv_w, conv_b, dt_bias, a_log, ssm_d, ssm_norm)
    cond = jnp.concatenate([c_ctx[None, :], c, jnp.zeros((COND_PAD - N_COND, D_MODEL), f32)], axis=0)
    mod = _ada(cond, w_ada, b_ada)
    x = jnp.concatenate([x_prompt.reshape(N_CTX_TOK, D_MODEL), x_sample.reshape(N_LAT_TOK, D_MODEL)], axis=0)
    h0 = state_ssm.reshape(DEC_BATCH, DEPTH, 2, SSD_XW, SSM_STATE)
    ks, vs, hs = [], [], []
    for l in range(DEPTH):
        x1, h2, k_l, v_l, h_l = _mix_layer(x, mod, l, prm, cache_k, cache_v, h0)
        ks.append(k_l)
        vs.append(v_l)
        hs.append(h_l)
        i = l // 2
        if l % 2 == 0:
            f = _ffn(h2, ffn_w1[i], ffn_w3[i], ffn_w2[i])
        else:
            f = _moe(h2, router_w[i], router_b[i], moe_w1[i], moe_w3[i], moe_w2[i])
        x = _resid(x1, f, mod, prm['g_post2'], l)
    y_p = x[:N_CTX_TOK].reshape(BATCH, SEQ, D_MODEL)
    y_s = x[N_CTX_TOK:].reshape(DEC_BATCH, DEC_SEQ, D_MODEL)
    return (y_p, y_s, jnp.stack(ks, axis=1), jnp.stack(vs, axis=1), jnp.stack(hs, axis=1))
```

```python
import functools

import numpy as np
import jax
import jax.numpy as jnp
from jax import lax
from jax.experimental import pallas as pl
from jax.experimental.pallas import tpu as pltpu

D_MODEL = 2048
BATCH = 16
SEQ = 256
DEPTH = 4
DEC_BATCH = 4
DEC_SEQ = 4096
PAST_LEN = 512
GRID_W = 64
POOL_WIDTH = D_MODEL // 4
ATTN_WIDTH = D_MODEL // 2
SSM_INNER = D_MODEL // 4
POOL_WINDOWS = (2, 4, 8, 16)
POOL_GROUPS = len(POOL_WINDOWS)
POOL_CH = POOL_WIDTH // POOL_GROUPS
HEAD_DIM = 128
N_HEADS = ATTN_WIDTH // HEAD_DIM
N_KV_HEADS = 2
GQA = N_HEADS // N_KV_HEADS
KV_WIDTH = N_KV_HEADS * HEAD_DIM
ATTN_WINDOW = 128
ATTN_BLOCK = 128
ATTN_SPAN = ATTN_BLOCK + 2 * ATTN_WINDOW
ATTN_SCALE = HEAD_DIM ** -0.5
ROPE_PAIRS = HEAD_DIM // 4
ROPE_BASE = 10000.0
SSM_HEADDIM = 64
SSM_HEADS = SSM_INNER // SSM_HEADDIM
SSM_GROUPS = 2
SSM_STATE = 128
SSM_CONV = 5
SSM_CHUNK = 128
CONV_DIM = SSM_INNER + 2 * SSM_GROUPS * SSM_STATE
D_FF = 7168
N_EXPERTS = 8
TOP_K = 2
RMS_EPS = 1e-6

N_CTX_TOK = BATCH * SEQ
N_LAT_TOK = DEC_BATCH * DEC_SEQ
N_TOK = N_CTX_TOK + N_LAT_TOK
N_COND = 1 + DEC_BATCH
COND_PAD = 8
assert N_CTX_TOK == DEC_SEQ

COL_U = 0
COL_Q = COL_U + POOL_WIDTH
COL_K = COL_Q + ATTN_WIDTH
COL_V = COL_K + KV_WIDTH
COL_Z = COL_V + KV_WIDTH
COL_XBC = COL_Z + SSM_INNER
PROJ_W = COL_XBC + CONV_DIM
DT_W = 2 * SSM_HEADS
LANE = 128
HALO = 8

V7X_VMEM_LIMIT = 56 * 1024 * 1024

FFN_TM = 1024
FFN_TF = 256
MOE_TM = 1024
PROJ_TM = 1024
PROJ_TN = 512
OUT_TM = 512
SEQ_TM = SEQ
ADA_TN = 1024

f32 = jnp.float32
bf16 = jnp.bfloat16


def _params(n_axes, **kw):
    return pltpu.CompilerParams(dimension_semantics=("arbitrary",) * n_axes,
                                vmem_limit_bytes=V7X_VMEM_LIMIT, **kw)


def _silu(x):
    return x * jax.nn.sigmoid(x)


def _rms(x):
    return x * lax.rsqrt(jnp.mean(x * x, axis=-1, keepdims=True) + RMS_EPS)


def _cond_row(tm):
    return lambda i: (i * tm) // DEC_SEQ


def _mod_spec(layer, tm, chunk):
    row = _cond_row(tm)
    return pl.BlockSpec((None, None, 1, D_MODEL), lambda i, *_: (layer, row(i), 0, chunk))


def _vec_spec(layer, width=None, col=0):
    return pl.BlockSpec((None, 1, width), lambda *_: (layer, 0, col))


def _ada_kernel(c_ref, w_ref, b_ref, o_ref):
    s = _silu(c_ref[...]).astype(bf16)
    o_ref[...] = jnp.dot(s, w_ref[...].astype(bf16), preferred_element_type=f32) + b_ref[...]


def _ada(cond, w_ada, b_ada):
    n = 6 * D_MODEL
    out = pl.pallas_call(
        _ada_kernel,
        grid=(DEPTH, n // ADA_TN),
        in_specs=[
            pl.BlockSpec((COND_PAD, D_MODEL), lambda l, j: (0, 0)),
            pl.BlockSpec((None, D_MODEL, ADA_TN), lambda l, j: (l, 0, j)),
            pl.BlockSpec((None, 1, ADA_TN), lambda l, j: (l, 0, j)),
        ],
        out_specs=pl.BlockSpec((None, COND_PAD, ADA_TN), lambda l, j: (l, 0, j)),
        out_shape=jax.ShapeDtypeStruct((DEPTH, COND_PAD, n), f32),
        compiler_params=_params(2),
        name="adaln_mod",
    )(cond, w_ada, b_ada.reshape(DEPTH, 1, n))
    return out.reshape(DEPTH, COND_PAD, 1, n)


def _inproj_kernel(x_ref, sh_ref, sc_ref, g_ref, w_ref, wdt_ref, o_ref, dt_ref, h_s):
    @pl.when(pl.program_id(1) == 0)
    def _():
        h = (_rms(x_ref[...]) * g_ref[...]) * (1.0 + sc_ref[...]) + sh_ref[...]
        hb = h.astype(bf16)
        h_s[...] = hb
        dt_ref[...] = jnp.dot(hb, wdt_ref[...], preferred_element_type=f32)

    o_ref[...] = jnp.dot(h_s[...], w_ref[...], preferred_element_type=f32)


def _inproj(x, mod, g_pre, w_main, w_dt, layer):
    tm, tn = PROJ_TM, PROJ_TN
    return pl.pallas_call(
        _inproj_kernel,
        grid=(N_TOK // tm, PROJ_W // tn),
        in_specs=[
            pl.BlockSpec((tm, D_MODEL), lambda i, j: (i, 0)),
            _mod_spec(layer, tm, 0),
            _mod_spec(layer, tm, 1),
            _vec_spec(layer, D_MODEL),
            pl.BlockSpec((None, D_MODEL, tn), lambda i, j: (layer, 0, j)),
            pl.BlockSpec((None, D_MODEL, LANE), lambda i, j: (layer, 0, 0)),
        ],
        out_specs=[
            pl.BlockSpec((tm, tn), lambda i, j: (i, j)),
            pl.BlockSpec((tm, LANE), lambda i, j: (i, 0)),
        ],
        out_shape=[jax.ShapeDtypeStruct((N_TOK, PROJ_W), f32),
                   jax.ShapeDtypeStruct((N_TOK, LANE), f32)],
        scratch_shapes=[pltpu.VMEM((tm, D_MODEL), bf16)],
        compiler_params=_params(2),
        name="in_proj",
    )(x, mod, mod, g_pre, w_main, w_dt)


def _seq_tile_position(i):
    is_ctx = i < N_CTX_TOK // SEQ_TM
    t0 = jnp.where(is_ctx, 0, ((i - N_CTX_TOK // SEQ_TM) * SEQ_TM) % DEC_SEQ)
    seq_len = jnp.where(is_ctx, SEQ, DEC_SEQ)
    return t0, seq_len


def _fill_extended(e_s, cur_ref, prev_ref, next_ref, t0, seq_len):
    first = t0 == 0
    last = t0 + SEQ_TM == seq_len
    e_s[0:HALO, :] = jnp.where(first, 0.0, prev_ref[...])
    e_s[HALO:HALO + SEQ_TM, :] = cur_ref[...]
    e_s[HALO + SEQ_TM:, :] = jnp.where(last, 0.0, next_ref[...])


def _halo_specs(width, col):
    per = SEQ_TM // HALO
    n8 = N_TOK // HALO
    return [
        pl.BlockSpec((SEQ_TM, width), lambda i, *a: (i, col(*a))),
        pl.BlockSpec((HALO, width), lambda i, *a: (jnp.maximum(i * per - 1, 0), col(*a))),
        pl.BlockSpec((HALO, width), lambda i, *a: (jnp.minimum((i + 1) * per, n8 - 1), col(*a))),
    ]


def _pool_kernel(cur_ref, prev_ref, next_ref, w_ref, sc_ref, o_ref, e_s):
    t0, seq_len = _seq_tile_position(pl.program_id(0))
    _fill_extended(e_s, cur_ref, prev_ref, next_ref, t0, seq_len)
    t = t0 + lax.broadcasted_iota(jnp.int32, (SEQ_TM, 1), 0)
    for g, w in enumerate(POOL_WINDOWS):
        lanes = pl.ds(g * POOL_CH, POOL_CH)
        acc = e_s[pl.ds(HALO - w // 2, SEQ_TM), lanes]
        for s in range(1 - w // 2, w // 2):
            acc = acc + e_s[pl.ds(HALO + s, SEQ_TM), lanes]
        cnt = jnp.minimum(t + w // 2, seq_len) - jnp.maximum(t - w // 2, 0)
        mean = acc / cnt.astype(f32)
        d = (mean - e_s[pl.ds(HALO, SEQ_TM), lanes]).astype(bf16)
        y = jnp.dot(d, w_ref[g].astype(bf16), preferred_element_type=f32) * sc_ref[:, g * POOL_CH:(g + 1) * POOL_CH]
        o_ref[:, g * POOL_CH:(g + 1) * POOL_CH] = y.astype(bf16)


def _pool(proj, pool_w, pool_scale, layer):
    return pl.pallas_call(
        _pool_kernel,
        grid=(N_TOK // SEQ_TM,),
        in_specs=_halo_specs(POOL_WIDTH, lambda: COL_U // POOL_WIDTH) + [
            pl.BlockSpec((None, POOL_GROUPS, POOL_CH, POOL_CH), lambda i: (layer, 0, 0, 0)),
            _vec_spec(layer, POOL_WIDTH),
        ],
        out_specs=pl.BlockSpec((SEQ_TM, POOL_WIDTH), lambda i: (i, 0)),
        out_shape=jax.ShapeDtypeStruct((N_TOK, POOL_WIDTH), bf16),
        scratch_shapes=[pltpu.VMEM((SEQ_TM + 2 * HALO, POOL_WIDTH), f32)],
        compiler_params=_params(1),
        name="pool_mixer",
    )(proj, proj, proj, pool_w, pool_scale)


CONV_TN = 512


def _conv_kernel(cur_ref, prev_ref, next_ref, w_ref, b_ref, o_ref, e_s):
    t0, seq_len = _seq_tile_position(pl.program_id(0))
    _fill_extended(e_s, cur_ref, prev_ref, next_ref, t0, seq_len)
    acc = b_ref[...] + e_s[pl.ds(HALO - SSM_CONV // 2, SEQ_TM), :] * w_ref[0:1, :]
    for k in range(1, SSM_CONV):
        acc = acc + e_s[pl.ds(HALO + k - SSM_CONV // 2, SEQ_TM), :] * w_ref[k:k + 1, :]
    o_ref[...] = _silu(acc)


def _conv(proj, conv_w, conv_b, layer):
    col0 = COL_XBC // CONV_TN
    return pl.pallas_call(
        _conv_kernel,
        grid=(N_TOK // SEQ_TM, CONV_DIM // CONV_TN),
        in_specs=_halo_specs(CONV_TN, lambda j: col0 + j) + [
            pl.BlockSpec((None, SSM_CONV, CONV_TN), lambda i, j: (layer, 0, j)),
            pl.BlockSpec((None, 1, CONV_TN), lambda i, j: (layer, 0, j)),
        ],
        out_specs=pl.BlockSpec((SEQ_TM, CONV_TN), lambda i, j: (i, j)),
        out_shape=jax.ShapeDtypeStruct((N_TOK, CONV_DIM), f32),
        scratch_shapes=[pltpu.VMEM((SEQ_TM + 2 * HALO, CONV_TN), f32)],
        compiler_params=_params(2),
        name="ssm_conv",
    )(proj, proj, proj, conv_w, conv_b)


def _softmax_sink(s, sink):
    m = jnp.maximum(jnp.max(s, axis=-1, keepdims=True), sink)
    e = jnp.exp(s - m)
    return e / (jnp.sum(e, axis=-1, keepdims=True) + jnp.exp(sink - m))


def _qk(q, k):
    return lax.dot_general(q, k, (((1,), (1,)), ((), ())), preferred_element_type=f32)


def _ctx_attn_kernel(sink_ref, q_ref, k_ref, v_ref, o_ref):
    kvh = pl.program_id(1)
    k = k_ref[...].astype(bf16)
    v = v_ref[...].astype(bf16)
    for g in range(GQA):
        q = q_ref[:, g * HEAD_DIM:(g + 1) * HEAD_DIM].astype(bf16)
        p = _softmax_sink(_qk(q, k) * ATTN_SCALE, sink_ref[kvh * GQA + g])
        o = jnp.dot(p.astype(bf16), v, preferred_element_type=f32)
        o_ref[:, g * HEAD_DIM:(g + 1) * HEAD_DIM] = o.astype(bf16)


def _ctx_attention(proj, sink):
    qw = GQA * HEAD_DIM
    return pl.pallas_call(
        _ctx_attn_kernel,
        grid=(BATCH, N_KV_HEADS),
        in_specs=[
            pl.BlockSpec(memory_space=pltpu.SMEM),
            pl.BlockSpec((SEQ, qw), lambda b, h: (b, COL_Q // qw + h)),
            pl.BlockSpec((SEQ, HEAD_DIM), lambda b, h: (b, COL_K // HEAD_DIM + h)),
            pl.BlockSpec((SEQ, HEAD_DIM), lambda b, h: (b, COL_V // HEAD_DIM + h)),
        ],
        out_specs=pl.BlockSpec((SEQ, qw), lambda b, h: (b, h)),
        out_shape=jax.ShapeDtypeStruct((N_CTX_TOK, ATTN_WIDTH), bf16),
        compiler_params=_params(2),
        name="ctx_attention",
    )(sink, proj, proj, proj)


def _rope_tables():
    t = np.arange(DEC_SEQ)
    inv = ROPE_BASE ** (-np.arange(ROPE_PAIRS, dtype=np.float64) / ROPE_PAIRS)
    ang_r = (t // GRID_W)[:, None] * inv
    ang_c = (t % GRID_W)[:, None] * inv
    cos = np.concatenate([np.cos(ang_r), np.cos(ang_r), np.cos(ang_c), np.cos(ang_c)], axis=1)
    sin = np.concatenate([-np.sin(ang_r), np.sin(ang_r), -np.sin(ang_c), np.sin(ang_c)], axis=1)
    return jnp.asarray(cos, f32), jnp.asarray(sin, f32)


def _rope(x, cos, sin):
    lane = lax.broadcasted_iota(jnp.int32, x.shape, 1)
    partner = jnp.where(lane % (2 * ROPE_PAIRS) < ROPE_PAIRS,
                        pltpu.roll(x, HEAD_DIM - ROPE_PAIRS, 1), pltpu.roll(x, ROPE_PAIRS, 1))
    return x * cos + partner * sin


ROPE_ROWS = 512


def _lat_attn_kernel(sink_ref, q0_ref, q1_ref, k_ref, v_ref, ck_ref, cv_ref, cosq_ref, sinq_ref,
                     cos_ref, sin_ref, o_ref, k_s, v_s, ck_s, cv_s):
    qb = pl.program_id(1)

    @pl.when(qb == 0)
    def _():
        zeros = jnp.zeros((ATTN_WINDOW, KV_WIDTH), bf16)
        k_s[0:ATTN_WINDOW, :] = zeros
        v_s[0:ATTN_WINDOW, :] = zeros
        k_s[ATTN_WINDOW + DEC_SEQ:, :] = zeros
        v_s[ATTN_WINDOW + DEC_SEQ:, :] = zeros

        def stage(c, carry):
            r = pl.multiple_of(c * ROPE_ROWS, ROPE_ROWS)
            cos = cos_ref[pl.ds(r, ROPE_ROWS), :]
            sin = sin_ref[pl.ds(r, ROPE_ROWS), :]
            for h in range(N_KV_HEADS):
                kh = k_ref[pl.ds(r, ROPE_ROWS), h * HEAD_DIM:(h + 1) * HEAD_DIM]
                k_s[pl.ds(ATTN_WINDOW + r, ROPE_ROWS), h * HEAD_DIM:(h + 1) * HEAD_DIM] = _rope(kh, cos, sin).astype(bf16)
            v_s[pl.ds(ATTN_WINDOW + r, ROPE_ROWS), :] = v_ref[pl.ds(r, ROPE_ROWS), :].astype(bf16)
            return carry

        lax.fori_loop(0, DEC_SEQ // ROPE_ROWS, stage, 0)
        ck_s[...] = ck_ref[...].astype(bf16)
        cv_s[...] = cv_ref[...].astype(bf16)

    start = pl.multiple_of(qb * ATTN_BLOCK, ATTN_BLOCK)
    rows = GQA * ATTN_BLOCK
    a = lax.broadcasted_iota(jnp.int32, (rows, ATTN_SPAN), 0) % ATTN_BLOCK
    b = lax.broadcasted_iota(jnp.int32, (rows, ATTN_SPAN), 1)
    pos = start - ATTN_WINDOW + b
    mask = (b >= a) & (b <= a + 2 * ATTN_WINDOW) & (pos >= 0) & (pos < DEC_SEQ)
    head_of_row = lax.broadcasted_iota(jnp.int32, (rows, 1), 0) // ATTN_BLOCK
    cos = cosq_ref[...]
    sin = sinq_ref[...]
    for h, q_ref in enumerate((q0_ref, q1_ref)):
        q = jnp.concatenate(
            [_rope(q_ref[:, g * HEAD_DIM:(g + 1) * HEAD_DIM], cos, sin) for g in range(GQA)], axis=0).astype(bf16)
        sink = jnp.zeros((rows, 1), f32)
        for g in range(GQA):
            sink = jnp.where(head_of_row == g, sink_ref[h * GQA + g], sink)
        lanes = pl.ds(h * HEAD_DIM, HEAD_DIM)
        kw = k_s[pl.ds(start, ATTN_SPAN), lanes]
        vw = v_s[pl.ds(start, ATTN_SPAN), lanes]
        s_win = jnp.where(mask, _qk(q, kw) * ATTN_SCALE, -jnp.inf)
        s_ctx = _qk(q, ck_s[:, lanes]) * ATTN_SCALE
        m = jnp.maximum(jnp.maximum(jnp.max(s_win, axis=-1, keepdims=True),
                                    jnp.max(s_ctx, axis=-1, keepdims=True)), sink)
        e_win = jnp.exp(s_win - m)
        e_ctx = jnp.exp(s_ctx - m)
        den = (jnp.sum(e_win, axis=-1, keepdims=True) + jnp.sum(e_ctx, axis=-1, keepdims=True)
               + jnp.exp(sink - m))
        o = (jnp.dot((e_win / den).astype(bf16), vw, preferred_element_type=f32)
             + jnp.dot((e_ctx / den).astype(bf16), cv_s[:, lanes], preferred_element_type=f32))
        for g in range(GQA):
            col = (h * GQA + g) * HEAD_DIM
            o_ref[:, col:col + HEAD_DIM] = o[g * ATTN_BLOCK:(g + 1) * ATTN_BLOCK].astype(bf16)


def _lat_attention(proj, cache_k, cache_v, sink, layer):
    cos, sin = _rope_tables()
    qw = GQA * HEAD_DIM
    nqb = DEC_SEQ // ATTN_BLOCK
    row0 = N_CTX_TOK // ATTN_BLOCK
    seq0 = N_CTX_TOK // DEC_SEQ
    ck = cache_k.reshape(DEC_BATCH, DEPTH, PAST_LEN, KV_WIDTH)
    cv = cache_v.reshape(DEC_BATCH, DEPTH, PAST_LEN, KV_WIDTH)
    ctx_spec = pl.BlockSpec((None, None, PAST_LEN, KV_WIDTH), lambda b, n: (b, layer, 0, 0))
    return pl.pallas_call(
        _lat_attn_kernel,
        grid=(DEC_BATCH, nqb),
        in_specs=[
            pl.BlockSpec(memory_space=pltpu.SMEM),
            pl.BlockSpec((ATTN_BLOCK, qw), lambda b, n: (row0 + b * nqb + n, COL_Q // qw)),
            pl.BlockSpec((ATTN_BLOCK, qw), lambda b, n: (row0 + b * nqb + n, COL_Q // qw + 1)),
            pl.BlockSpec((DEC_SEQ, KV_WIDTH), lambda b, n: (seq0 + b, COL_K // KV_WIDTH)),
            pl.BlockSpec((DEC_SEQ, KV_WIDTH), lambda b, n: (seq0 + b, COL_V // KV_WIDTH)),
            ctx_spec, ctx_spec,
            pl.BlockSpec((ATTN_BLOCK, HEAD_DIM), lambda b, n: (n, 0)),
            pl.BlockSpec((ATTN_BLOCK, HEAD_DIM), lambda b, n: (n, 0)),
            pl.BlockSpec((DEC_SEQ, HEAD_DIM), lambda b, n: (0, 0)),
            pl.BlockSpec((DEC_SEQ, HEAD_DIM), lambda b, n: (0, 0)),
        ],
        out_specs=pl.BlockSpec((ATTN_BLOCK, ATTN_WIDTH), lambda b, n: (b * nqb + n, 0)),
        out_shape=jax.ShapeDtypeStruct((N_LAT_TOK, ATTN_WIDTH), bf16),
        scratch_shapes=[
            pltpu.VMEM((DEC_SEQ + 2 * ATTN_WINDOW, KV_WIDTH), bf16),
            pltpu.VMEM((DEC_SEQ + 2 * ATTN_WINDOW, KV_WIDTH), bf16),
            pltpu.VMEM((PAST_LEN, KV_WIDTH), bf16),
            pltpu.VMEM((PAST_LEN, KV_WIDTH), bf16),
        ],
        compiler_params=_params(2),
        name="latent_attention",
    )(sink, proj, proj, proj, proj, ck, cv, cos, sin, cos, sin)


SSD_XW = SSM_HEADS * SSM_HEADDIM
SSD_LW = SSM_HEADS * LANE
PAIR_W = 2 * SSM_HEADDIM
GROUP_W = SSD_XW // SSM_GROUPS


def _split3(x):
    hi = x.astype(bf16)
    r = x - hi.astype(f32)
    mid = r.astype(bf16)
    lo = (r - mid.astype(f32)).astype(bf16)
    return hi, mid, lo


def _dot_exact_rhs(sel, x):
    return sum(jnp.dot(sel, p, preferred_element_type=f32) for p in _split3(x))


def _dot_exact_lhs(x, sel):
    return sum(jnp.dot(p, sel, preferred_element_type=f32) for p in _split3(x))


def _ssd_constants():
    q = SSM_CHUNK
    lower = np.tril(np.ones((q, q), np.float32))
    tri = np.stack([lower, lower.T])
    x_wide = np.zeros((2, LANE, SSD_LW), np.float32)
    x_head = np.zeros((2, LANE, SSD_XW), np.float32)
    for d in range(2):
        for j in range(SSM_HEADS):
            x_wide[d, d * SSM_HEADS + j, j * LANE:(j + 1) * LANE] = 1.0
            x_head[d, d * SSM_HEADS + j, j * SSM_HEADDIM:(j + 1) * SSM_HEADDIM] = 1.0
    return jnp.asarray(tri, bf16), jnp.asarray(x_wide, bf16), jnp.asarray(x_head, bf16)


def _ssd_direction(d, act, dt_raw, h_s, bias, a_neg, tri_ref, xw_ref, xh_ref):
    q = SSM_CHUNK
    dt = jnp.logaddexp(dt_raw + bias, 0.0)
    a = dt * a_neg
    cs = _dot_exact_rhs(tri_ref[d], a)
    cs_wide = _dot_exact_lhs(cs, xw_ref[d])
    cs_head = _dot_exact_lhs(cs, xh_ref[d])
    dt_head = _dot_exact_lhs(dt, xh_ref[d])
    cs_t = cs.T
    edge = cs_head[q - 1:q, :] if d == 0 else cs_head[0:1, :]
    xd = act[:, 0:SSD_XW] * dt_head
    xdb = xd.astype(bf16)
    xdw = (xd * jnp.exp(edge - cs_head)).astype(bf16)
    li = lax.broadcasted_iota(jnp.int32, (q, q), 0)
    si = lax.broadcasted_iota(jnp.int32, (q, q), 1)
    seen = (li >= si) if d == 0 else (li <= si)
    lane = lax.broadcasted_iota(jnp.int32, (q, PAIR_W), 1)
    h_in = h_s[...]
    ys = []
    for g in range(SSM_GROUPS):
        bm = act[:, SSD_XW + g * SSM_STATE:SSD_XW + (g + 1) * SSM_STATE]
        cm = act[:, SSD_XW + (SSM_GROUPS + g) * SSM_STATE:SSD_XW + (SSM_GROUPS + g + 1) * SSM_STATE]
        cmb = cm.astype(bf16)
        cb = _qk(cmb, bm.astype(bf16))
        cols = slice(g * GROUP_W, (g + 1) * GROUP_W)
        y_off = jnp.dot(cmb, h_in[:, cols].astype(bf16), preferred_element_type=f32) * jnp.exp(cs_head[:, cols])
        y_diag = []
        for pr in range(GROUP_W // PAIR_W):
            pcols = slice(g * GROUP_W + pr * PAIR_W, g * GROUP_W + (pr + 1) * PAIR_W)
            halves = []
            for jj in range(2):
                j = (g * GROUP_W + pr * PAIR_W) // SSM_HEADDIM + jj
                row = d * SSM_HEADS + j
                seg = cs_wide[:, j * LANE:(j + 1) * LANE] - cs_t[row:row + 1, :]
                m = (cb * jnp.exp(jnp.where(seen, seg, -jnp.inf))).astype(bf16)
                halves.append(jnp.dot(m, xdb[:, pcols], preferred_element_type=f32))
            y_diag.append(jnp.where(lane < SSM_HEADDIM, halves[0], halves[1]))
        ys.append(jnp.concatenate(y_diag, axis=1) + y_off)
        st = jnp.dot(bm.T.astype(bf16), xdw[:, cols], preferred_element_type=f32)
        h_s[:, cols] = h_in[:, cols] * jnp.exp(edge[:, cols]) + st
    return jnp.concatenate(ys, axis=1)


def _ssd_kernel(*refs, has_h0, emit_state):
    refs = list(refs)
    af_ref, ab_ref, dtf_ref, dtb_ref = refs[:4]
    refs = refs[4:]
    h0_ref = refs.pop(0) if has_h0 else None
    bias_ref, alog_ref, tri_ref, xw_ref, xh_ref = refs[:5]
    refs = refs[5:]
    yf_ref, yb_ref = refs[:2]
    refs = refs[2:]
    hout_ref = refs.pop(0) if emit_state else None
    hf_s, hb_s = refs
    c = pl.program_id(1)
    blocks = SSD_XW // LANE

    @pl.when(c == 0)
    def _():
        for d, h_s in enumerate((hf_s, hb_s)):
            if has_h0:
                for k in range(blocks):
                    h_s[:, k * LANE:(k + 1) * LANE] = h0_ref[d, k * LANE:(k + 1) * LANE, :].T
            else:
                h_s[...] = jnp.zeros_like(h_s)

    lane = lax.broadcasted_iota(jnp.int32, (1, LANE), 1)
    a_neg = jnp.where(lane < DT_W, -jnp.exp(alog_ref[...]), 0.0)
    bias = bias_ref[...]
    yf_ref[...] = _ssd_direction(0, af_ref[...], dtf_ref[...], hf_s, bias, a_neg, tri_ref, xw_ref, xh_ref)
    yb_ref[...] = _ssd_direction(1, ab_ref[...], dtb_ref[...], hb_s, bias, a_neg, tri_ref, xw_ref, xh_ref)

    if emit_state:
        @pl.when(c == pl.num_programs(1) - 1)
        def _():
            for d, h_s in enumerate((hf_s, hb_s)):
                for k in range(blocks):
                    hout_ref[d, k * LANE:(k + 1) * LANE, :] = h_s[:, k * LANE:(k + 1) * LANE].T


def _ssd(act, dt_raw, h0, dt_bias, a_log, layer, *, n_seq, seq_len, row0, emit_state):
    q = SSM_CHUNK
    nc = seq_len // q
    blk0 = row0 // q
    tri, x_wide, x_head = _ssd_constants()
    fwd = lambda s, c: (blk0 + s * nc + c, 0)
    bwd = lambda s, c: (blk0 + s * nc + nc - 1 - c, 0)
    const3 = lambda s, c: (0, 0, 0)
    in_specs = [
        pl.BlockSpec((q, CONV_DIM), fwd), pl.BlockSpec((q, CONV_DIM), bwd),
        pl.BlockSpec((q, LANE), fwd), pl.BlockSpec((q, LANE), bwd),
    ]
    args = [act, act, dt_raw, dt_raw]
    if h0 is not None:
        in_specs.append(pl.BlockSpec((None, None, 2, SSD_XW, SSM_STATE), lambda s, c: (s, layer, 0, 0, 0)))
        args.append(h0)
    in_specs += [
        _vec_spec(layer, LANE), _vec_spec(layer, LANE),
        pl.BlockSpec(tri.shape, const3), pl.BlockSpec(x_wide.shape, const3), pl.BlockSpec(x_head.shape, const3),
    ]
    args += [dt_bias, a_log, tri, x_wide, x_head]
    rows = n_seq * seq_len
    out_specs = [pl.BlockSpec((q, SSD_XW), lambda s, c: (s * nc + c, 0)),
                 pl.BlockSpec((q, SSD_XW), lambda s, c: (s * nc + nc - 1 - c, 0))]
    out_shape = [jax.ShapeDtypeStruct((rows, SSD_XW), f32)] * 2
    if emit_state:
        out_specs.append(pl.BlockSpec((None, 2, SSD_XW, SSM_STATE), lambda s, c: (s, 0, 0, 0)))
        out_shape.append(jax.ShapeDtypeStruct((n_seq, 2, SSD_XW, SSM_STATE), f32))
    return pl.pallas_call(
        functools.partial(_ssd_kernel, has_h0=h0 is not None, emit_state=emit_state),
        grid=(n_seq, nc),
        in_specs=in_specs,
        out_specs=out_specs,
        out_shape=out_shape,
        scratch_shapes=[pltpu.VMEM((SSM_STATE, SSD_XW), f32), pltpu.VMEM((SSM_STATE, SSD_XW), f32)],
        compiler_params=_params(2),
        name="ssd_scan",
    )(*args)


OUT_TN = 512
OUT_NT = D_MODEL // OUT_TN


def _outproj_kernel(yp_ref, o_ref, yf_ref, yb_ref, xs_ref, z_ref, x_ref, gate_ref, sh2_ref, sc2_ref,
                    dx_ref, gssm_ref, gpost_ref, gpre2_ref, w_ref, x1_ref, h2_ref, a_s, m_s):
    j = pl.program_id(1)

    @pl.when(j == 0)
    def _():
        y = (yf_ref[...] + yb_ref[...] + xs_ref[...] * dx_ref[...]) * _silu(z_ref[...])
        a_s[:, 0:POOL_WIDTH] = yp_ref[...]
        a_s[:, POOL_WIDTH:POOL_WIDTH + ATTN_WIDTH] = o_ref[...]
        a_s[:, POOL_WIDTH + ATTN_WIDTH:] = (_rms(y) * gssm_ref[...]).astype(bf16)

    m_s[j] = jnp.dot(a_s[...], w_ref[...], preferred_element_type=f32)

    @pl.when(j == OUT_NT - 1)
    def _():
        ss = sum(jnp.sum(m_s[k] * m_s[k], axis=-1, keepdims=True) for k in range(OUT_NT))
        r = lax.rsqrt(ss / D_MODEL + RMS_EPS)
        ss1 = jnp.zeros_like(ss)
        for k in range(OUT_NT):
            c = slice(k * OUT_TN, (k + 1) * OUT_TN)
            x1 = x_ref[:, c] + gate_ref[:, c] * ((m_s[k] * r) * gpost_ref[:, c])
            x1_ref[:, c] = x1
            ss1 = ss1 + jnp.sum(x1 * x1, axis=-1, keepdims=True)
        r1 = lax.rsqrt(ss1 / D_MODEL + RMS_EPS)
        for k in range(OUT_NT):
            c = slice(k * OUT_TN, (k + 1) * OUT_TN)
            h2 = ((x1_ref[:, c] * r1) * gpre2_ref[:, c]) * (1.0 + sc2_ref[:, c]) + sh2_ref[:, c]
            h2_ref[:, c] = h2.astype(h2_ref.dtype)


def _outproj(y_pool, o_attn, y_f, y_b, act, proj, x, mod, d_skip, g_ssm, g_post, g_pre2, w_out, layer, h2_dtype):
    tm = OUT_TM
    row = lambda i, j: (i, 0)
    return pl.pallas_call(
        _outproj_kernel,
        grid=(N_TOK // tm, OUT_NT),
        in_specs=[
            pl.BlockSpec((tm, POOL_WIDTH), row),
            pl.BlockSpec((tm, ATTN_WIDTH), row),
            pl.BlockSpec((tm, SSM_INNER), row),
            pl.BlockSpec((tm, SSM_INNER), row),
            pl.BlockSpec((tm, SSM_INNER), row),
            pl.BlockSpec((tm, SSM_INNER), lambda i, j: (i, COL_Z // SSM_INNER)),
            pl.BlockSpec((tm, D_MODEL), row),
            _mod_spec(layer, tm, 2), _mod_spec(layer, tm, 3), _mod_spec(layer, tm, 4),
            _vec_spec(layer, SSM_INNER), _vec_spec(layer, SSM_INNER),
            _vec_spec(layer, D_MODEL), _vec_spec(layer, D_MODEL),
            pl.BlockSpec((None, D_MODEL, OUT_TN), lambda i, j: (layer, 0, j)),
        ],
        out_specs=[pl.BlockSpec((tm, D_MODEL), row), pl.BlockSpec((tm, D_MODEL), row)],
        out_shape=[jax.ShapeDtypeStruct((N_TOK, D_MODEL), f32), jax.ShapeDtypeStruct((N_TOK, D_MODEL), h2_dtype)],
        scratch_shapes=[pltpu.VMEM((tm, D_MODEL), bf16), pltpu.VMEM((OUT_NT, tm, OUT_TN), f32)],
        compiler_params=_params(2),
        name="out_proj",
    )(y_pool, o_attn, y_f, y_b, act, proj, x, mod, mod, mod, d_skip, g_ssm, g_post, g_pre2, w_out)


def _swiglu_step(h, w1_ref, w3_ref, w2_ref):
    a = jnp.dot(h, w1_ref[...].astype(bf16), preferred_element_type=f32)
    b = jnp.dot(h, w3_ref[...].astype(bf16), preferred_element_type=f32)
    g = (_silu(a) * b).astype(bf16)
    return jnp.dot(g, w2_ref[...].astype(bf16), preferred_element_type=f32)


def _ffn_kernel(h_ref, w1_ref, w3_ref, w2_ref, o_ref):
    @pl.when(pl.program_id(1) == 0)
    def _():
        o_ref[...] = jnp.zeros_like(o_ref)

    o_ref[...] += _swiglu_step(h_ref[...], w1_ref, w3_ref, w2_ref)


def _ffn(h, w1, w3, w2):
    T, D = h.shape
    F = w1.shape[1]
    return pl.pallas_call(
        _ffn_kernel,
        grid=(T // FFN_TM, F // FFN_TF),
        in_specs=[
            pl.BlockSpec((FFN_TM, D), lambda i, j: (i, 0)),
            pl.BlockSpec((D, FFN_TF), lambda i, j: (0, j)),
            pl.BlockSpec((D, FFN_TF), lambda i, j: (0, j)),
            pl.BlockSpec((FFN_TF, D), lambda i, j: (j, 0)),
        ],
        out_specs=pl.BlockSpec((FFN_TM, D), lambda i, j: (i, 0)),
        out_shape=jax.ShapeDtypeStruct((T, D), f32),
        compiler_params=_params(2),
        name="ffn_swiglu",
    )(h, w1, w3, w2)


MOE_TK = N_TOK * TOP_K
MOE_BLOCKS = MOE_TK // MOE_TM + N_EXPERTS
MOE_ROWS = MOE_BLOCKS * MOE_TM
SLOT_BITS = 16
assert MOE_TK + MOE_TM <= 1 << SLOT_BITS and N_TOK << SLOT_BITS < 1 << 31
DMA_UNROLL = 8


def _moe_kernel(be_ref, nv_ref, idx_ref, h_hbm, w1_ref, w3_ref, w2_ref, y_hbm, xf_s, xb_s, acc_s, sem):
    i = pl.program_id(0)
    j = pl.program_id(1)
    base = i * MOE_TM

    @pl.when(jnp.logical_and(i == 0, j == 0))
    def _():
        acc_s[...] = jnp.zeros_like(acc_s)
        spill = pltpu.make_async_copy(acc_s, y_hbm.at[pl.ds(MOE_TK, MOE_TM), :], sem.at[1])
        spill.start()
        spill.wait()

    @pl.when(i < nv_ref[0])
    def _():
        @pl.when(j == 0)
        def _():
            def gather(r, carry):
                tok = lax.shift_right_logical(idx_ref[base + r], SLOT_BITS)
                pltpu.make_async_copy(h_hbm.at[pl.ds(tok, 1), :], xf_s.at[pl.ds(r, 1), :], sem.at[0]).start()
                return carry

            lax.fori_loop(0, MOE_TM, gather, 0, unroll=DMA_UNROLL)
            pltpu.make_async_copy(h_hbm.at[pl.ds(0, MOE_TM), :], xf_s, sem.at[0]).wait()
            xb_s[...] = xf_s[...].astype(bf16)
            acc_s[...] = jnp.zeros_like(acc_s)

        acc_s[...] += _swiglu_step(xb_s[...], w1_ref, w3_ref, w2_ref)

        @pl.when(j == pl.num_programs(1) - 1)
        def _():
            def scatter(r, carry):
                slot = idx_ref[base + r] & ((1 << SLOT_BITS) - 1)
                pltpu.make_async_copy(acc_s.at[pl.ds(r, 1), :], y_hbm.at[pl.ds(slot, 1), :], sem.at[1]).start()
                return carry

            lax.fori_loop(0, MOE_TM, scatter, 0, unroll=DMA_UNROLL)
            pltpu.make_async_copy(acc_s, y_hbm.at[pl.ds(0, MOE_TM), :], sem.at[1]).wait()


def _moe_experts(row_idx, block_e, n_valid, h, w1, w3, w2):
    D = D_MODEL
    nf = D_FF // FFN_TF

    def jeff(i, j, nv):
        return jnp.where(i < nv[0], j, nf - 1)

    return pl.pallas_call(
        _moe_kernel,
        grid_spec=pltpu.PrefetchScalarGridSpec(
            num_scalar_prefetch=3,
            grid=(MOE_BLOCKS, nf),
            in_specs=[
                pl.BlockSpec(memory_space=pl.ANY),
                pl.BlockSpec((None, D, FFN_TF), lambda i, j, be, nv, ix: (be[i], 0, jeff(i, j, nv))),
                pl.BlockSpec((None, D, FFN_TF), lambda i, j, be, nv, ix: (be[i], 0, jeff(i, j, nv))),
                pl.BlockSpec((None, FFN_TF, D), lambda i, j, be, nv, ix: (be[i], jeff(i, j, nv), 0)),
            ],
            out_specs=pl.BlockSpec(memory_space=pl.ANY),
            scratch_shapes=[pltpu.VMEM((MOE_TM, D), f32), pltpu.VMEM((MOE_TM, D), bf16),
                            pltpu.VMEM((MOE_TM, D), f32), pltpu.SemaphoreType.DMA((2,))],
        ),
        out_shape=jax.ShapeDtypeStruct((MOE_TK + MOE_TM, D), f32),
        compiler_params=_params(2, disable_bounds_checks=True),
        name="moe_swiglu",
    )(block_e, n_valid, row_idx, h, w1, w3, w2)


ROUTE_TM = 1024
NEG_BIG = -1e30


def _router_kernel(h_ref, w_ref, b_ref, e_ref, g_ref):
    logits = jnp.dot(h_ref[...].astype(bf16), w_ref[...], preferred_element_type=f32) + b_ref[...]
    lane = lax.broadcasted_iota(jnp.int32, logits.shape, 1)
    m1 = jnp.max(logits, axis=-1, keepdims=True)
    i1 = jnp.min(jnp.where(logits == m1, lane, LANE), axis=-1, keepdims=True)
    rest = jnp.where(lane == i1, -jnp.inf, logits)
    m2 = jnp.max(rest, axis=-1, keepdims=True)
    i2 = jnp.min(jnp.where(rest == m2, lane, LANE), axis=-1, keepdims=True)
    e2 = jnp.exp(m2 - m1)
    g1 = 1.0 / (1.0 + e2)
    e_ref[...] = jnp.where(lane == 0, i1, jnp.where(lane == 1, i2, 0))
    g_ref[...] = jnp.where(lane == 0, g1, jnp.where(lane == 1, e2 * g1, 0.0))


def _router(h, router_w, router_b):
    w = jnp.pad(router_w, ((0, 0), (0, LANE - N_EXPERTS))).astype(bf16)
    b = jnp.pad(router_b, (0, LANE - N_EXPERTS), constant_values=NEG_BIG)[None, :]
    tm = ROUTE_TM
    return pl.pallas_call(
        _router_kernel,
        grid=(N_TOK // tm,),
        in_specs=[pl.BlockSpec((tm, D_MODEL), lambda i: (i, 0)),
                  pl.BlockSpec((D_MODEL, LANE), lambda i: (0, 0)),
                  pl.BlockSpec((1, LANE), lambda i: (0, 0))],
        out_specs=[pl.BlockSpec((tm, LANE), lambda i: (i, 0)), pl.BlockSpec((tm, LANE), lambda i: (i, 0))],
        out_shape=[jax.ShapeDtypeStruct((N_TOK, LANE), jnp.int32), jax.ShapeDtypeStruct((N_TOK, LANE), f32)],
        compiler_params=_params(1),
        name="moe_router",
    )(h, w, b)


def _moe_plan(top_e):
    flat_e = top_e.reshape(-1)
    onehot = (flat_e[:, None] == jnp.arange(N_EXPERTS, dtype=jnp.int32)[None, :]).astype(jnp.int32)
    rank = jnp.cumsum(onehot, axis=0) - onehot
    counts = jnp.sum(onehot, axis=0)
    padded = (counts + MOE_TM - 1) // MOE_TM * MOE_TM
    pad_ends = jnp.cumsum(padded)
    pad_starts = pad_ends - padded
    dest = jnp.sum(onehot * (pad_starts[None, :] + rank), axis=1)
    pair = jnp.arange(MOE_TK, dtype=jnp.int32)
    packed = ((pair // TOP_K) << SLOT_BITS) | pair
    spill = MOE_TK + jnp.arange(MOE_ROWS, dtype=jnp.int32) % MOE_TM
    row_idx = spill.at[dest].set(packed)
    n_valid = (pad_ends[-1] // MOE_TM).astype(jnp.int32)
    blocks = jnp.arange(MOE_BLOCKS, dtype=jnp.int32)
    block_e = jnp.sum((blocks[:, None] * MOE_TM >= pad_ends[None, :]).astype(jnp.int32), axis=1)
    block_e = jnp.minimum(block_e, N_EXPERTS - 1)
    last_e = block_e[jnp.maximum(n_valid - 1, 0)]
    block_e = jnp.where(blocks < n_valid, block_e, last_e)
    return row_idx, block_e, n_valid.reshape(1)


def _moe(h, router_w, router_b, w1, w3, w2):
    top_e, gates = _router(h, router_w, router_b)
    row_idx, block_e, n_valid = _moe_plan(top_e[:, :TOP_K])
    y = _moe_experts(row_idx, block_e, n_valid, h, w1, w3, w2)
    return y.reshape((MOE_TK + MOE_TM) // TOP_K, TOP_K * D_MODEL), gates


RES_TM = 512


def _resid_kernel(x_ref, f_ref, gate_ref, g_ref, o_ref):
    o_ref[...] = x_ref[...] + gate_ref[...] * (_rms(f_ref[...]) * g_ref[...])


def _resid_moe_kernel(x_ref, y_ref, gw_ref, gate_ref, g_ref, o_ref):
    f = gw_ref[:, 0:1] * y_ref[:, 0:D_MODEL]
    for k in range(1, TOP_K):
        f = f + gw_ref[:, k:k + 1] * y_ref[:, k * D_MODEL:(k + 1) * D_MODEL]
    o_ref[...] = x_ref[...] + gate_ref[...] * (_rms(f) * g_ref[...])


def _resid(x, f, mod, g_post, layer, gates=None):
    tm = RES_TM
    row = lambda i: (i, 0)
    body, args = _resid_kernel, [x, f]
    in_specs = [pl.BlockSpec((tm, D_MODEL), row), pl.BlockSpec((tm, f.shape[1]), row)]
    if gates is not None:
        body = _resid_moe_kernel
        args.append(gates)
        in_specs.append(pl.BlockSpec((tm, LANE), row))
    return pl.pallas_call(
        body,
        grid=(N_TOK // tm,),
        in_specs=in_specs + [_mod_spec(layer, tm, 5), _vec_spec(layer, D_MODEL)],
        out_specs=pl.BlockSpec((tm, D_MODEL), row),
        out_shape=jax.ShapeDtypeStruct((N_TOK, D_MODEL), f32),
        compiler_params=_params(1),
        name="ffn_residual",
    )(*args, mod, g_post)


def _row_params(p, width=None):
    p = p.reshape(DEPTH, -1)
    if width is not None and p.shape[1] < width:
        p = jnp.pad(p, ((0, 0), (0, width - p.shape[1])))
    return p[:, None, :]


def _mix_layer(x, mod, layer, prm, cache_k, cache_v, state_ssm, h2_dtype=bf16):
    proj, dt_raw = _inproj(x, mod, prm['g_pre1'], prm['w_in'], prm['w_dt'], layer)
    y_pool = _pool(proj, prm['pool_w'], prm['pool_scale'], layer)
    act = _conv(proj, prm['conv_w'], prm['conv_b'], layer)
    sink = prm['attn_sink'][layer]
    o_ctx = _ctx_attention(proj, sink)
    o_lat = _lat_attention(proj, cache_k, cache_v, sink, layer)
    o_attn = jnp.concatenate([o_ctx, o_lat], axis=0)
    yf_c, yb_c, h_ctx = _ssd(act, dt_raw, None, prm['dt_bias'], prm['a_log'], layer,
                             n_seq=BATCH, seq_len=SEQ, row0=0, emit_state=True)
    yf_l, yb_l = _ssd(act, dt_raw, state_ssm, prm['dt_bias'], prm['a_log'], layer,
                      n_seq=DEC_BATCH, seq_len=DEC_SEQ, row0=N_CTX_TOK, emit_state=False)
    y_f = jnp.concatenate([yf_c, yf_l], axis=0)
    y_b = jnp.concatenate([yb_c, yb_l], axis=0)
    x1, h2 = _outproj(y_pool, o_attn, y_f, y_b, act, proj, x, mod, prm['d_skip'], prm['g_ssm'],
                      prm['g_post1'], prm['g_pre2'], prm['w_out'], layer, h2_dtype)
    k_ctx = proj[:N_CTX_TOK, COL_K:COL_K + KV_WIDTH].reshape(BATCH, SEQ, N_KV_HEADS, HEAD_DIM)
    v_ctx = proj[:N_CTX_TOK, COL_V:COL_V + KV_WIDTH].reshape(BATCH, SEQ, N_KV_HEADS, HEAD_DIM)
    h_ctx = h_ctx.reshape(BATCH, 2, SSM_HEADS, SSM_HEADDIM, SSM_STATE)
    return x1, h2, k_ctx, v_ctx, h_ctx


def _prepare(norm_mix_pre, norm_mix_post, norm_ffn_pre, norm_ffn_post, w_in, w_out, pool_w, pool_scale,
             attn_sink, conv_w, conv_b, dt_bias, a_log, ssm_d, ssm_norm):
    w_dt = jnp.pad(w_in[:, :, PROJ_W:], ((0, 0), (0, 0), (0, LANE - DT_W)))
    return dict(
        g_pre1=_row_params(norm_mix_pre), g_post1=_row_params(norm_mix_post),
        g_pre2=_row_params(norm_ffn_pre), g_post2=_row_params(norm_ffn_post),
        w_in=w_in[:, :, :PROJ_W].astype(bf16), w_dt=w_dt.astype(bf16), w_out=w_out.astype(bf16),
        pool_w=pool_w, pool_scale=_row_params(pool_scale), attn_sink=attn_sink,
        conv_w=conv_w, conv_b=_row_params(conv_b),
        dt_bias=_row_params(dt_bias, LANE), a_log=_row_params(a_log, LANE),
        d_skip=_row_params(jnp.repeat(ssm_d, SSM_HEADDIM, axis=1)), g_ssm=_row_params(ssm_norm),
    )


def kernel(x_prompt, x_sample, cache_k, cache_v, state_ssm, c, c_ctx, w_ada, b_ada, norm_mix_pre,
           norm_mix_post, norm_ffn_pre, norm_ffn_post, w_in, w_out, pool_w, pool_scale, attn_sink,
           conv_w, conv_b, dt_bias, a_log, ssm_d, ssm_norm, ffn_w1, ffn_w3, ffn_w2, router_w, router_b,
           moe_w1, moe_w3, moe_w2):
    prm = _prepare(norm_mix_pre, norm_mix_post, norm_ffn_pre, norm_ffn_post, w_in, w_out, pool_w, pool_scale,
                   attn_sink, conv_w, conv_b, dt_bias, a_log, ssm_d, ssm_norm)
    cond = jnp.concatenate([c_ctx[None, :], c, jnp.zeros((COND_PAD - N_COND, D_MODEL), f32)], axis=0)
    mod = _ada(cond, w_ada, b_ada)
    x = jnp.concatenate([x_prompt.reshape(N_CTX_TOK, D_MODEL), x_sample.reshape(N_LAT_TOK, D_MODEL)], axis=0)
    h0 = state_ssm.reshape(DEC_BATCH, DEPTH, 2, SSD_XW, SSM_STATE)
    ks, vs, hs = [], [], []
    for l in range(DEPTH):
        dense = l % 2 == 0
        x1, h2, k_l, v_l, h_l = _mix_layer(x, mod, l, prm, cache_k, cache_v, h0, bf16 if dense else f32)
        ks.append(k_l)
        vs.append(v_l)
        hs.append(h_l)
        i = l // 2
        if dense:
            f = _ffn(h2, ffn_w1[i], ffn_w3[i], ffn_w2[i])
            x = _resid(x1, f, mod, prm['g_post2'], l)
        else:
            y, gates = _moe(h2, router_w[i], router_b[i], moe_w1[i], moe_w3[i], moe_w2[i])
            x = _resid(x1, y, mod, prm['g_post2'], l, gates)
    y_p = x[:N_CTX_TOK].reshape(BATCH, SEQ, D_MODEL)
    y_s = x[N_CTX_TOK:].reshape(DEC_BATCH, DEC_SEQ, D_MODEL)
    return (y_p, y_s, jnp.stack(ks, axis=1), jnp.stack(vs, axis=1), jnp.stack(hs, axis=1))
```

```python
import functools

import numpy as np
import jax
import jax.numpy as jnp
from jax import lax
from jax.experimental import pallas as pl
from jax.experimental.pallas import tpu as pltpu

D_MODEL = 2048
BATCH = 16
SEQ = 256
DEPTH = 4
DEC_BATCH = 4
DEC_SEQ = 4096
PAST_LEN = 512
GRID_W = 64
POOL_WIDTH = D_MODEL // 4
ATTN_WIDTH = D_MODEL // 2
SSM_INNER = D_MODEL // 4
POOL_WINDOWS = (2, 4, 8, 16)
POOL_GROUPS = len(POOL_WINDOWS)
POOL_CH = POOL_WIDTH // POOL_GROUPS
HEAD_DIM = 128
N_HEADS = ATTN_WIDTH // HEAD_DIM
N_KV_HEADS = 2
GQA = N_HEADS // N_KV_HEADS
KV_WIDTH = N_KV_HEADS * HEAD_DIM
ATTN_WINDOW = 128
ATTN_BLOCK = 128
ATTN_SPAN = ATTN_BLOCK + 2 * ATTN_WINDOW
ATTN_SCALE = HEAD_DIM ** -0.5
ROPE_PAIRS = HEAD_DIM // 4
ROPE_BASE = 10000.0
SSM_HEADDIM = 64
SSM_HEADS = SSM_INNER // SSM_HEADDIM
SSM_GROUPS = 2
SSM_STATE = 128
SSM_CONV = 5
SSM_CHUNK = 128
CONV_DIM = SSM_INNER + 2 * SSM_GROUPS * SSM_STATE
D_FF = 7168
N_EXPERTS = 8
TOP_K = 2
RMS_EPS = 1e-6

N_CTX_TOK = BATCH * SEQ
N_LAT_TOK = DEC_BATCH * DEC_SEQ
N_TOK = N_CTX_TOK + N_LAT_TOK
N_COND = 1 + DEC_BATCH
COND_PAD = 8
assert N_CTX_TOK == DEC_SEQ

COL_U = 0
COL_Q = COL_U + POOL_WIDTH
COL_K = COL_Q + ATTN_WIDTH
COL_V = COL_K + KV_WIDTH
COL_Z = COL_V + KV_WIDTH
COL_XBC = COL_Z + SSM_INNER
PROJ_W = COL_XBC + CONV_DIM
DT_W = 2 * SSM_HEADS
LANE = 128
HALO = 8

V7X_VMEM_LIMIT = 56 * 1024 * 1024

FFN_TM = 1024
FFN_TF = 256
MOE_TM = 1024
PROJ_TM = 1024
PROJ_TN = 512
OUT_TM = 512
SEQ_TM = SEQ
ADA_TN = 1024

f32 = jnp.float32
bf16 = jnp.bfloat16


def _params(n_axes, **kw):
    return pltpu.CompilerParams(dimension_semantics=("arbitrary",) * n_axes,
                                vmem_limit_bytes=V7X_VMEM_LIMIT, **kw)


def _silu(x):
    return x * jax.nn.sigmoid(x)


def _rms(x):
    return x * lax.rsqrt(jnp.mean(x * x, axis=-1, keepdims=True) + RMS_EPS)


def _cond_row(tm):
    return lambda i: (i * tm) // DEC_SEQ


def _mod_spec(layer, tm, chunk):
    row = _cond_row(tm)
    return pl.BlockSpec((None, None, 1, D_MODEL), lambda i, *_: (layer, row(i), 0, chunk))


def _vec_spec(layer, width=None, col=0):
    return pl.BlockSpec((None, 1, width), lambda *_: (layer, 0, col))


def _ada_kernel(c_ref, w_ref, b_ref, o_ref):
    s = _silu(c_ref[...]).astype(bf16)
    o_ref[...] = jnp.dot(s, w_ref[...].astype(bf16), preferred_element_type=f32) + b_ref[...]


def _ada(cond, w_ada, b_ada):
    n = 6 * D_MODEL
    out = pl.pallas_call(
        _ada_kernel,
        grid=(DEPTH, n // ADA_TN),
        in_specs=[
            pl.BlockSpec((COND_PAD, D_MODEL), lambda l, j: (0, 0)),
            pl.BlockSpec((None, D_MODEL, ADA_TN), lambda l, j: (l, 0, j)),
            pl.BlockSpec((None, 1, ADA_TN), lambda l, j: (l, 0, j)),
        ],
        out_specs=pl.BlockSpec((None, COND_PAD, ADA_TN), lambda l, j: (l, 0, j)),
        out_shape=jax.ShapeDtypeStruct((DEPTH, COND_PAD, n), f32),
        compiler_params=_params(2),
        name="adaln_mod",
    )(cond, w_ada, b_ada.reshape(DEPTH, 1, n))
    return out.reshape(DEPTH, COND_PAD, 1, n)


def _modulated_norm(x, g_ref, sc_ref, sh_ref):
    return (_rms(x) * g_ref[...]) * (1.0 + sc_ref[...]) + sh_ref[...]


PRE_TM = 512


def _prenorm_kernel(x_ref, sh_ref, sc_ref, g_ref, h_ref):
    h_ref[...] = _modulated_norm(x_ref[...], g_ref, sc_ref, sh_ref).astype(bf16)


def _prenorm(x, mod, g_pre, layer):
    tm = PRE_TM
    row = lambda i: (i, 0)
    return pl.pallas_call(
        _prenorm_kernel,
        grid=(N_TOK // tm,),
        in_specs=[pl.BlockSpec((tm, D_MODEL), row), _mod_spec(layer, tm, 0), _mod_spec(layer, tm, 1),
                  _vec_spec(layer, D_MODEL)],
        out_specs=pl.BlockSpec((tm, D_MODEL), row),
        out_shape=jax.ShapeDtypeStruct((N_TOK, D_MODEL), bf16),
        compiler_params=_params(1),
        name="mix_prenorm",
    )(x, mod, mod, g_pre)


def _inproj_kernel(h_ref, w_ref, wdt_ref, o_ref, dt_ref):
    @pl.when(pl.program_id(1) == 0)
    def _():
        dt_ref[...] = jnp.dot(h_ref[...], wdt_ref[...], preferred_element_type=f32)

    o_ref[...] = jnp.dot(h_ref[...], w_ref[...], preferred_element_type=f32)


def _inproj(h, w_main, w_dt, layer):
    tm, tn = PROJ_TM, PROJ_TN
    return pl.pallas_call(
        _inproj_kernel,
        grid=(N_TOK // tm, PROJ_W // tn),
        in_specs=[
            pl.BlockSpec((tm, D_MODEL), lambda i, j: (i, 0)),
            pl.BlockSpec((None, D_MODEL, tn), lambda i, j: (layer, 0, j)),
            pl.BlockSpec((None, D_MODEL, LANE), lambda i, j: (layer, 0, 0)),
        ],
        out_specs=[
            pl.BlockSpec((tm, tn), lambda i, j: (i, j)),
            pl.BlockSpec((tm, LANE), lambda i, j: (i, 0)),
        ],
        out_shape=[jax.ShapeDtypeStruct((N_TOK, PROJ_W), f32),
                   jax.ShapeDtypeStruct((N_TOK, LANE), f32)],
        compiler_params=_params(2),
        name="in_proj",
    )(h, w_main, w_dt)


def _seq_tile_position(i):
    is_ctx = i < N_CTX_TOK // SEQ_TM
    t0 = jnp.where(is_ctx, 0, ((i - N_CTX_TOK // SEQ_TM) * SEQ_TM) % DEC_SEQ)
    seq_len = jnp.where(is_ctx, SEQ, DEC_SEQ)
    return t0, seq_len


def _fill_extended(e_s, cur_ref, prev_ref, next_ref, t0, seq_len):
    first = t0 == 0
    last = t0 + SEQ_TM == seq_len
    e_s[0:HALO, :] = jnp.where(first, 0.0, prev_ref[...])
    e_s[HALO:HALO + SEQ_TM, :] = cur_ref[...]
    e_s[HALO + SEQ_TM:, :] = jnp.where(last, 0.0, next_ref[...])


def _halo_specs(width, col):
    per = SEQ_TM // HALO
    n8 = N_TOK // HALO
    return [
        pl.BlockSpec((SEQ_TM, width), lambda i, *a: (i, col(*a))),
        pl.BlockSpec((HALO, width), lambda i, *a: (jnp.maximum(i * per - 1, 0), col(*a))),
        pl.BlockSpec((HALO, width), lambda i, *a: (jnp.minimum((i + 1) * per, n8 - 1), col(*a))),
    ]


def _pool_kernel(cur_ref, prev_ref, next_ref, w_ref, sc_ref, o_ref, e_s):
    t0, seq_len = _seq_tile_position(pl.program_id(0))
    _fill_extended(e_s, cur_ref, prev_ref, next_ref, t0, seq_len)
    t = t0 + lax.broadcasted_iota(jnp.int32, (SEQ_TM, 1), 0)
    for g, w in enumerate(POOL_WINDOWS):
        lanes = pl.ds(g * POOL_CH, POOL_CH)
        acc = e_s[pl.ds(HALO - w // 2, SEQ_TM), lanes]
        for s in range(1 - w // 2, w // 2):
            acc = acc + e_s[pl.ds(HALO + s, SEQ_TM), lanes]
        cnt = jnp.minimum(t + w // 2, seq_len) - jnp.maximum(t - w // 2, 0)
        mean = acc / cnt.astype(f32)
        d = (mean - e_s[pl.ds(HALO, SEQ_TM), lanes]).astype(bf16)
        y = jnp.dot(d, w_ref[g].astype(bf16), preferred_element_type=f32) * sc_ref[:, g * POOL_CH:(g + 1) * POOL_CH]
        o_ref[:, g * POOL_CH:(g + 1) * POOL_CH] = y.astype(bf16)


def _pool(proj, pool_w, pool_scale, layer):
    return pl.pallas_call(
        _pool_kernel,
        grid=(N_TOK // SEQ_TM,),
        in_specs=_halo_specs(POOL_WIDTH, lambda: COL_U // POOL_WIDTH) + [
            pl.BlockSpec((None, POOL_GROUPS, POOL_CH, POOL_CH), lambda i: (layer, 0, 0, 0)),
            _vec_spec(layer, POOL_WIDTH),
        ],
        out_specs=pl.BlockSpec((SEQ_TM, POOL_WIDTH), lambda i: (i, 0)),
        out_shape=jax.ShapeDtypeStruct((N_TOK, POOL_WIDTH), bf16),
        scratch_shapes=[pltpu.VMEM((SEQ_TM + 2 * HALO, POOL_WIDTH), f32)],
        compiler_params=_params(1),
        name="pool_mixer",
    )(proj, proj, proj, pool_w, pool_scale)


CONV_TN = 512


def _conv_kernel(cur_ref, prev_ref, next_ref, w_ref, b_ref, o_ref, e_s):
    t0, seq_len = _seq_tile_position(pl.program_id(0))
    _fill_extended(e_s, cur_ref, prev_ref, next_ref, t0, seq_len)
    acc = b_ref[...] + e_s[pl.ds(HALO - SSM_CONV // 2, SEQ_TM), :] * w_ref[0:1, :]
    for k in range(1, SSM_CONV):
        acc = acc + e_s[pl.ds(HALO + k - SSM_CONV // 2, SEQ_TM), :] * w_ref[k:k + 1, :]
    o_ref[...] = _silu(acc)


def _conv(proj, conv_w, conv_b, layer):
    col0 = COL_XBC // CONV_TN
    return pl.pallas_call(
        _conv_kernel,
        grid=(N_TOK // SEQ_TM, CONV_DIM // CONV_TN),
        in_specs=_halo_specs(CONV_TN, lambda j: col0 + j) + [
            pl.BlockSpec((None, SSM_CONV, CONV_TN), lambda i, j: (layer, 0, j)),
            pl.BlockSpec((None, 1, CONV_TN), lambda i, j: (layer, 0, j)),
        ],
        out_specs=pl.BlockSpec((SEQ_TM, CONV_TN), lambda i, j: (i, j)),
        out_shape=jax.ShapeDtypeStruct((N_TOK, CONV_DIM), f32),
        scratch_shapes=[pltpu.VMEM((SEQ_TM + 2 * HALO, CONV_TN), f32)],
        compiler_params=_params(2),
        name="ssm_conv",
    )(proj, proj, proj, conv_w, conv_b)


def _softmax_sink(s, sink):
    m = jnp.maximum(jnp.max(s, axis=-1, keepdims=True), sink)
    e = jnp.exp(s - m)
    return e / (jnp.sum(e, axis=-1, keepdims=True) + jnp.exp(sink - m))


def _qk(q, k):
    return lax.dot_general(q, k, (((1,), (1,)), ((), ())), preferred_element_type=f32)


def _ctx_attn_kernel(sink_ref, q_ref, k_ref, v_ref, o_ref):
    kvh = pl.program_id(1)
    k = k_ref[...].astype(bf16)
    v = v_ref[...].astype(bf16)
    for g in range(GQA):
        q = q_ref[:, g * HEAD_DIM:(g + 1) * HEAD_DIM].astype(bf16)
        p = _softmax_sink(_qk(q, k) * ATTN_SCALE, sink_ref[kvh * GQA + g])
        o = jnp.dot(p.astype(bf16), v, preferred_element_type=f32)
        o_ref[:, g * HEAD_DIM:(g + 1) * HEAD_DIM] = o.astype(bf16)


def _ctx_attention(proj, sink):
    qw = GQA * HEAD_DIM
    return pl.pallas_call(
        _ctx_attn_kernel,
        grid=(BATCH, N_KV_HEADS),
        in_specs=[
            pl.BlockSpec(memory_space=pltpu.SMEM),
            pl.BlockSpec((SEQ, qw), lambda b, h: (b, COL_Q // qw + h)),
            pl.BlockSpec((SEQ, HEAD_DIM), lambda b, h: (b, COL_K // HEAD_DIM + h)),
            pl.BlockSpec((SEQ, HEAD_DIM), lambda b, h: (b, COL_V // HEAD_DIM + h)),
        ],
        out_specs=pl.BlockSpec((SEQ, qw), lambda b, h: (b, h)),
        out_shape=jax.ShapeDtypeStruct((N_CTX_TOK, ATTN_WIDTH), bf16),
        compiler_params=_params(2),
        name="ctx_attention",
    )(sink, proj, proj, proj)


def _rope_tables():
    t = np.arange(DEC_SEQ)
    inv = ROPE_BASE ** (-np.arange(ROPE_PAIRS, dtype=np.float64) / ROPE_PAIRS)
    ang_r = (t // GRID_W)[:, None] * inv
    ang_c = (t % GRID_W)[:, None] * inv
    cos = np.concatenate([np.cos(ang_r), np.cos(ang_r), np.cos(ang_c), np.cos(ang_c)], axis=1)
    sin = np.concatenate([-np.sin(ang_r), np.sin(ang_r), -np.sin(ang_c), np.sin(ang_c)], axis=1)
    return jnp.asarray(cos, f32), jnp.asarray(sin, f32)


def _rope(x, cos, sin):
    lane = lax.broadcasted_iota(jnp.int32, x.shape, 1)
    partner = jnp.where(lane % (2 * ROPE_PAIRS) < ROPE_PAIRS,
                        pltpu.roll(x, HEAD_DIM - ROPE_PAIRS, 1), pltpu.roll(x, ROPE_PAIRS, 1))
    return x * cos + partner * sin


ROPE_ROWS = 512


def _lat_attn_kernel(sink_ref, q0_ref, q1_ref, k_ref, v_ref, ck_ref, cv_ref, cosq_ref, sinq_ref,
                     cos_ref, sin_ref, o_ref, k_s, v_s, ck_s, cv_s):
    qb = pl.program_id(1)

    @pl.when(qb == 0)
    def _():
        zeros = jnp.zeros((ATTN_WINDOW, KV_WIDTH), bf16)
        k_s[0:ATTN_WINDOW, :] = zeros
        v_s[0:ATTN_WINDOW, :] = zeros
        k_s[ATTN_WINDOW + DEC_SEQ:, :] = zeros
        v_s[ATTN_WINDOW + DEC_SEQ:, :] = zeros

        def stage(c, carry):
            r = pl.multiple_of(c * ROPE_ROWS, ROPE_ROWS)
            cos = cos_ref[pl.ds(r, ROPE_ROWS), :]
            sin = sin_ref[pl.ds(r, ROPE_ROWS), :]
            for h in range(N_KV_HEADS):
                kh = k_ref[pl.ds(r, ROPE_ROWS), h * HEAD_DIM:(h + 1) * HEAD_DIM]
                k_s[pl.ds(ATTN_WINDOW + r, ROPE_ROWS), h * HEAD_DIM:(h + 1) * HEAD_DIM] = _rope(kh, cos, sin).astype(bf16)
            v_s[pl.ds(ATTN_WINDOW + r, ROPE_ROWS), :] = v_ref[pl.ds(r, ROPE_ROWS), :].astype(bf16)
            return carry

        lax.fori_loop(0, DEC_SEQ // ROPE_ROWS, stage, 0)
        ck_s[...] = ck_ref[...].astype(bf16)
        cv_s[...] = cv_ref[...].astype(bf16)

    start = pl.multiple_of(qb * ATTN_BLOCK, ATTN_BLOCK)
    rows = GQA * ATTN_BLOCK
    a = lax.broadcasted_iota(jnp.int32, (rows, ATTN_SPAN), 0) % ATTN_BLOCK
    b = lax.broadcasted_iota(jnp.int32, (rows, ATTN_SPAN), 1)
    pos = start - ATTN_WINDOW + b
    mask = (b >= a) & (b <= a + 2 * ATTN_WINDOW) & (pos >= 0) & (pos < DEC_SEQ)
    head_of_row = lax.broadcasted_iota(jnp.int32, (rows, 1), 0) // ATTN_BLOCK
    cos = cosq_ref[...]
    sin = sinq_ref[...]
    for h, q_ref in enumerate((q0_ref, q1_ref)):
        q = jnp.concatenate(
            [_rope(q_ref[:, g * HEAD_DIM:(g + 1) * HEAD_DIM], cos, sin) for g in range(GQA)], axis=0).astype(bf16)
        sink = jnp.zeros((rows, 1), f32)
        for g in range(GQA):
            sink = jnp.where(head_of_row == g, sink_ref[h * GQA + g], sink)
        lanes = pl.ds(h * HEAD_DIM, HEAD_DIM)
        kw = k_s[pl.ds(start, ATTN_SPAN), lanes]
        vw = v_s[pl.ds(start, ATTN_SPAN), lanes]
        s_win = jnp.where(mask, _qk(q, kw) * ATTN_SCALE, -jnp.inf)
        s_ctx = _qk(q, ck_s[:, lanes]) * ATTN_SCALE
        m = jnp.maximum(jnp.maximum(jnp.max(s_win, axis=-1, keepdims=True),
                                    jnp.max(s_ctx, axis=-1, keepdims=True)), sink)
        e_win = jnp.exp(s_win - m)
        e_ctx = jnp.exp(s_ctx - m)
        den = (jnp.sum(e_win, axis=-1, keepdims=True) + jnp.sum(e_ctx, axis=-1, keepdims=True)
               + jnp.exp(sink - m))
        o = (jnp.dot((e_win / den).astype(bf16), vw, preferred_element_type=f32)
             + jnp.dot((e_ctx / den).astype(bf16), cv_s[:, lanes], preferred_element_type=f32))
        for g in range(GQA):
            col = (h * GQA + g) * HEAD_DIM
            o_ref[:, col:col + HEAD_DIM] = o[g * ATTN_BLOCK:(g + 1) * ATTN_BLOCK].astype(bf16)


def _lat_attention(proj, cache_k, cache_v, sink, layer):
    cos, sin = _rope_tables()
    qw = GQA * HEAD_DIM
    nqb = DEC_SEQ // ATTN_BLOCK
    row0 = N_CTX_TOK // ATTN_BLOCK
    seq0 = N_CTX_TOK // DEC_SEQ
    ck = cache_k.reshape(DEC_BATCH, DEPTH, PAST_LEN, KV_WIDTH)
    cv = cache_v.reshape(DEC_BATCH, DEPTH, PAST_LEN, KV_WIDTH)
    ctx_spec = pl.BlockSpec((None, None, PAST_LEN, KV_WIDTH), lambda b, n: (b, layer, 0, 0))
    return pl.pallas_call(
        _lat_attn_kernel,
        grid=(DEC_BATCH, nqb),
        in_specs=[
            pl.BlockSpec(memory_space=pltpu.SMEM),
            pl.BlockSpec((ATTN_BLOCK, qw), lambda b, n: (row0 + b * nqb + n, COL_Q // qw)),
            pl.BlockSpec((ATTN_BLOCK, qw), lambda b, n: (row0 + b * nqb + n, COL_Q // qw + 1)),
            pl.BlockSpec((DEC_SEQ, KV_WIDTH), lambda b, n: (seq0 + b, COL_K // KV_WIDTH)),
            pl.BlockSpec((DEC_SEQ, KV_WIDTH), lambda b, n: (seq0 + b, COL_V // KV_WIDTH)),
            ctx_spec, ctx_spec,
            pl.BlockSpec((ATTN_BLOCK, HEAD_DIM), lambda b, n: (n, 0)),
            pl.BlockSpec((ATTN_BLOCK, HEAD_DIM), lambda b, n: (n, 0)),
            pl.BlockSpec((DEC_SEQ, HEAD_DIM), lambda b, n: (0, 0)),
            pl.BlockSpec((DEC_SEQ, HEAD_DIM), lambda b, n: (0, 0)),
        ],
        out_specs=pl.BlockSpec((ATTN_BLOCK, ATTN_WIDTH), lambda b, n: (b * nqb + n, 0)),
        out_shape=jax.ShapeDtypeStruct((N_LAT_TOK, ATTN_WIDTH), bf16),
        scratch_shapes=[
            pltpu.VMEM((DEC_SEQ + 2 * ATTN_WINDOW, KV_WIDTH), bf16),
            pltpu.VMEM((DEC_SEQ + 2 * ATTN_WINDOW, KV_WIDTH), bf16),
            pltpu.VMEM((PAST_LEN, KV_WIDTH), bf16),
            pltpu.VMEM((PAST_LEN, KV_WIDTH), bf16),
        ],
        compiler_params=_params(2),
        name="latent_attention",
    )(sink, proj, proj, proj, proj, ck, cv, cos, sin, cos, sin)


SSD_XW = SSM_HEADS * SSM_HEADDIM
SSD_LW = SSM_HEADS * LANE
PAIR_W = 2 * SSM_HEADDIM
GROUP_W = SSD_XW // SSM_GROUPS


def _split3(x):
    hi = x.astype(bf16)
    r = x - hi.astype(f32)
    mid = r.astype(bf16)
    lo = (r - mid.astype(f32)).astype(bf16)
    return hi, mid, lo


def _dot_exact_rhs(sel, x):
    return sum(jnp.dot(sel, p, preferred_element_type=f32) for p in _split3(x))


def _dot_exact_lhs(x, sel):
    return sum(jnp.dot(p, sel, preferred_element_type=f32) for p in _split3(x))


def _ssd_constants():
    q = SSM_CHUNK
    lower = np.tril(np.ones((q, q), np.float32))
    tri = np.stack([lower, lower.T])
    x_wide = np.zeros((2, LANE, SSD_LW), np.float32)
    x_head = np.zeros((2, LANE, SSD_XW), np.float32)
    for d in range(2):
        for j in range(SSM_HEADS):
            x_wide[d, d * SSM_HEADS + j, j * LANE:(j + 1) * LANE] = 1.0
            x_head[d, d * SSM_HEADS + j, j * SSM_HEADDIM:(j + 1) * SSM_HEADDIM] = 1.0
    return jnp.asarray(tri, bf16), jnp.asarray(x_wide, bf16), jnp.asarray(x_head, bf16)


def _ssd_direction(d, act, dt_raw, h_s, bias, a_neg, tri_ref, xw_ref, xh_ref):
    q = SSM_CHUNK
    dt = jnp.logaddexp(dt_raw + bias, 0.0)
    a = dt * a_neg
    cs = _dot_exact_rhs(tri_ref[d], a)
    cs_wide = _dot_exact_lhs(cs, xw_ref[d])
    cs_head = _dot_exact_lhs(cs, xh_ref[d])
    dt_head = _dot_exact_lhs(dt, xh_ref[d])
    cs_t = cs.T
    edge = cs_head[q - 1:q, :] if d == 0 else cs_head[0:1, :]
    xd = act[:, 0:SSD_XW] * dt_head
    xdb = xd.astype(bf16)
    xdw = (xd * jnp.exp(edge - cs_head)).astype(bf16)
    li = lax.broadcasted_iota(jnp.int32, (q, q), 0)
    si = lax.broadcasted_iota(jnp.int32, (q, q), 1)
    seen = (li >= si) if d == 0 else (li <= si)
    lane = lax.broadcasted_iota(jnp.int32, (q, PAIR_W), 1)
    h_in = h_s[...]
    ys = []
    for g in range(SSM_GROUPS):
        bm = act[:, SSD_XW + g * SSM_STATE:SSD_XW + (g + 1) * SSM_STATE]
        cm = act[:, SSD_XW + (SSM_GROUPS + g) * SSM_STATE:SSD_XW + (SSM_GROUPS + g + 1) * SSM_STATE]
        cmb = cm.astype(bf16)
        cb = _qk(cmb, bm.astype(bf16))
        cols = slice(g * GROUP_W, (g + 1) * GROUP_W)
        y_off = jnp.dot(cmb, h_in[:, cols].astype(bf16), preferred_element_type=f32) * jnp.exp(cs_head[:, cols])
        y_diag = []
        for pr in range(GROUP_W // PAIR_W):
            pcols = slice(g * GROUP_W + pr * PAIR_W, g * GROUP_W + (pr + 1) * PAIR_W)
            halves = []
            for jj in range(2):
                j = (g * GROUP_W + pr * PAIR_W) // SSM_HEADDIM + jj
                row = d * SSM_HEADS + j
                seg = cs_wide[:, j * LANE:(j + 1) * LANE] - cs_t[row:row + 1, :]
                m = (cb * jnp.exp(jnp.where(seen, seg, -jnp.inf))).astype(bf16)
                halves.append(jnp.dot(m, xdb[:, pcols], preferred_element_type=f32))
            y_diag.append(jnp.where(lane < SSM_HEADDIM, halves[0], halves[1]))
        ys.append(jnp.concatenate(y_diag, axis=1) + y_off)
        st = jnp.dot(bm.T.astype(bf16), xdw[:, cols], preferred_element_type=f32)
        h_s[:, cols] = h_in[:, cols] * jnp.exp(edge[:, cols]) + st
    return jnp.concatenate(ys, axis=1)


def _ssd_kernel(*refs, has_h0, emit_state):
    refs = list(refs)
    af_ref, ab_ref, dtf_ref, dtb_ref = refs[:4]
    refs = refs[4:]
    h0_ref = refs.pop(0) if has_h0 else None
    bias_ref, alog_ref, tri_ref, xw_ref, xh_ref = refs[:5]
    refs = refs[5:]
    yf_ref, yb_ref = refs[:2]
    refs = refs[2:]
    hout_ref = refs.pop(0) if emit_state else None
    hf_s, hb_s = refs
    c = pl.program_id(1)
    blocks = SSD_XW // LANE

    @pl.when(c == 0)
    def _():
        for d, h_s in enumerate((hf_s, hb_s)):
            if has_h0:
                for k in range(blocks):
                    h_s[:, k * LANE:(k + 1) * LANE] = h0_ref[d, k * LANE:(k + 1) * LANE, :].T
            else:
                h_s[...] = jnp.zeros_like(h_s)

    lane = lax.broadcasted_iota(jnp.int32, (1, LANE), 1)
    a_neg = jnp.where(lane < DT_W, -jnp.exp(alog_ref[...]), 0.0)
    bias = bias_ref[...]
    yf_ref[...] = _ssd_direction(0, af_ref[...], dtf_ref[...], hf_s, bias, a_neg, tri_ref, xw_ref, xh_ref)
    yb_ref[...] = _ssd_direction(1, ab_ref[...], dtb_ref[...], hb_s, bias, a_neg, tri_ref, xw_ref, xh_ref)

    if emit_state:
        @pl.when(c == pl.num_programs(1) - 1)
        def _():
            for d, h_s in enumerate((hf_s, hb_s)):
                for k in range(blocks):
                    hout_ref[d, k * LANE:(k + 1) * LANE, :] = h_s[:, k * LANE:(k + 1) * LANE].T


def _ssd(act, dt_raw, h0, dt_bias, a_log, layer, *, n_seq, seq_len, row0, emit_state):
    q = SSM_CHUNK
    nc = seq_len // q
    blk0 = row0 // q
    tri, x_wide, x_head = _ssd_constants()
    fwd = lambda s, c: (blk0 + s * nc + c, 0)
    bwd = lambda s, c: (blk0 + s * nc + nc - 1 - c, 0)
    const3 = lambda s, c: (0, 0, 0)
    in_specs = [
        pl.BlockSpec((q, CONV_DIM), fwd), pl.BlockSpec((q, CONV_DIM), bwd),
        pl.BlockSpec((q, LANE), fwd), pl.BlockSpec((q, LANE), bwd),
    ]
    args = [act, act, dt_raw, dt_raw]
    if h0 is not None:
        in_specs.append(pl.BlockSpec((None, None, 2, SSD_XW, SSM_STATE), lambda s, c: (s, layer, 0, 0, 0)))
        args.append(h0)
    in_specs += [
        _vec_spec(layer, LANE), _vec_spec(layer, LANE),
        pl.BlockSpec(tri.shape, const3), pl.BlockSpec(x_wide.shape, const3), pl.BlockSpec(x_head.shape, const3),
    ]
    args += [dt_bias, a_log, tri, x_wide, x_head]
    rows = n_seq * seq_len
    out_specs = [pl.BlockSpec((q, SSD_XW), lambda s, c: (s * nc + c, 0)),
                 pl.BlockSpec((q, SSD_XW), lambda s, c: (s * nc + nc - 1 - c, 0))]
    out_shape = [jax.ShapeDtypeStruct((rows, SSD_XW), f32)] * 2
    if emit_state:
        out_specs.append(pl.BlockSpec((None, 2, SSD_XW, SSM_STATE), lambda s, c: (s, 0, 0, 0)))
        out_shape.append(jax.ShapeDtypeStruct((n_seq, 2, SSD_XW, SSM_STATE), f32))
    return pl.pallas_call(
        functools.partial(_ssd_kernel, has_h0=h0 is not None, emit_state=emit_state),
        grid=(n_seq, nc),
        in_specs=in_specs,
        out_specs=out_specs,
        out_shape=out_shape,
        scratch_shapes=[pltpu.VMEM((SSM_STATE, SSD_XW), f32), pltpu.VMEM((SSM_STATE, SSD_XW), f32)],
        compiler_params=_params(2),
        name="ssd_scan",
    )(*args)


OUT_TN = 512
OUT_NT = D_MODEL // OUT_TN


def _outproj_kernel(yp_ref, oc_ref, ol_ref, yfc_ref, yfl_ref, ybc_ref, ybl_ref, xs_ref, z_ref, x_ref,
                    gate_ref, sh2_ref, sc2_ref, dx_ref, gssm_ref, gpost_ref, gpre2_ref, w_ref,
                    x1_ref, h2_ref, a_s, m_s):
    j = pl.program_id(1)

    @pl.when(j == 0)
    def _():
        is_ctx = pl.program_id(0) < N_CTX_TOK // OUT_TM
        yf = jnp.where(is_ctx, yfc_ref[...], yfl_ref[...])
        yb = jnp.where(is_ctx, ybc_ref[...], ybl_ref[...])
        y = (yf + yb + xs_ref[...] * dx_ref[...]) * _silu(z_ref[...])
        a_s[:, 0:POOL_WIDTH] = yp_ref[...]
        a_s[:, POOL_WIDTH:POOL_WIDTH + ATTN_WIDTH] = jnp.where(is_ctx, oc_ref[...], ol_ref[...])
        a_s[:, POOL_WIDTH + ATTN_WIDTH:] = (_rms(y) * gssm_ref[...]).astype(bf16)

    m_s[j] = jnp.dot(a_s[...], w_ref[...], preferred_element_type=f32)

    @pl.when(j == OUT_NT - 1)
    def _():
        ss = sum(jnp.sum(m_s[k] * m_s[k], axis=-1, keepdims=True) for k in range(OUT_NT))
        r = lax.rsqrt(ss / D_MODEL + RMS_EPS)
        ss1 = jnp.zeros_like(ss)
        for k in range(OUT_NT):
            c = slice(k * OUT_TN, (k + 1) * OUT_TN)
            x1 = x_ref[:, c] + gate_ref[:, c] * ((m_s[k] * r) * gpost_ref[:, c])
            x1_ref[:, c] = x1
            ss1 = ss1 + jnp.sum(x1 * x1, axis=-1, keepdims=True)
        r1 = lax.rsqrt(ss1 / D_MODEL + RMS_EPS)
        for k in range(OUT_NT):
            c = slice(k * OUT_TN, (k + 1) * OUT_TN)
            h2 = ((x1_ref[:, c] * r1) * gpre2_ref[:, c]) * (1.0 + sc2_ref[:, c]) + sh2_ref[:, c]
            h2_ref[:, c] = h2.astype(h2_ref.dtype)


def _outproj(y_pool, o_ctx, o_lat, yf_c, yf_l, yb_c, yb_l, act, proj, x, mod, d_skip, g_ssm, g_post, g_pre2,
             w_out, layer, h2_dtype):
    tm = OUT_TM
    n_ctx = N_CTX_TOK // tm
    row = lambda i, j: (i, 0)
    ctx = lambda i, j: (jnp.minimum(i, n_ctx - 1), 0)
    lat = lambda i, j: (jnp.maximum(i - n_ctx, 0), 0)
    return pl.pallas_call(
        _outproj_kernel,
        grid=(N_TOK // tm, OUT_NT),
        in_specs=[
            pl.BlockSpec((tm, POOL_WIDTH), row),
            pl.BlockSpec((tm, ATTN_WIDTH), ctx), pl.BlockSpec((tm, ATTN_WIDTH), lat),
            pl.BlockSpec((tm, SSM_INNER), ctx), pl.BlockSpec((tm, SSM_INNER), lat),
            pl.BlockSpec((tm, SSM_INNER), ctx), pl.BlockSpec((tm, SSM_INNER), lat),
            pl.BlockSpec((tm, SSM_INNER), row),
            pl.BlockSpec((tm, SSM_INNER), lambda i, j: (i, COL_Z // SSM_INNER)),
            pl.BlockSpec((tm, D_MODEL), row),
            _mod_spec(layer, tm, 2), _mod_spec(layer, tm, 3), _mod_spec(layer, tm, 4),
            _vec_spec(layer, SSM_INNER), _vec_spec(layer, SSM_INNER),
            _vec_spec(layer, D_MODEL), _vec_spec(layer, D_MODEL),
            pl.BlockSpec((None, D_MODEL, OUT_TN), lambda i, j: (layer, 0, j)),
        ],
        out_specs=[pl.BlockSpec((tm, D_MODEL), row), pl.BlockSpec((tm, D_MODEL), row)],
        out_shape=[jax.ShapeDtypeStruct((N_TOK, D_MODEL), f32), jax.ShapeDtypeStruct((N_TOK, D_MODEL), h2_dtype)],
        scratch_shapes=[pltpu.VMEM((tm, D_MODEL), bf16), pltpu.VMEM((OUT_NT, tm, OUT_TN), f32)],
        compiler_params=_params(2),
        name="out_proj",
    )(y_pool, o_ctx, o_lat, yf_c, yf_l, yb_c, yb_l, act, proj, x, mod, mod, mod, d_skip, g_ssm, g_post, g_pre2, w_out)


def _swiglu_step(h, w1_ref, w3_ref, w2_ref):
    a = jnp.dot(h, w1_ref[...].astype(bf16), preferred_element_type=f32)
    b = jnp.dot(h, w3_ref[...].astype(bf16), preferred_element_type=f32)
    g = (_silu(a) * b).astype(bf16)
    return jnp.dot(g, w2_ref[...].astype(bf16), preferred_element_type=f32)


def _ffn_kernel(h_ref, w1_ref, w3_ref, w2_ref, o_ref):
    @pl.when(pl.program_id(1) == 0)
    def _():
        o_ref[...] = jnp.zeros_like(o_ref)

    o_ref[...] += _swiglu_step(h_ref[...], w1_ref, w3_ref, w2_ref)


def _ffn(h, w1, w3, w2, n):
    T, D = h.shape
    F = w1.shape[2]
    return pl.pallas_call(
        _ffn_kernel,
        grid=(T // FFN_TM, F // FFN_TF),
        in_specs=[
            pl.BlockSpec((FFN_TM, D), lambda i, j: (i, 0)),
            pl.BlockSpec((None, D, FFN_TF), lambda i, j: (n, 0, j)),
            pl.BlockSpec((None, D, FFN_TF), lambda i, j: (n, 0, j)),
            pl.BlockSpec((None, FFN_TF, D), lambda i, j: (n, j, 0)),
        ],
        out_specs=pl.BlockSpec((FFN_TM, D), lambda i, j: (i, 0)),
        out_shape=jax.ShapeDtypeStruct((T, D), f32),
        compiler_params=_params(2),
        name="ffn_swiglu",
    )(h, w1, w3, w2)


MOE_TK = N_TOK * TOP_K
MOE_BLOCKS = MOE_TK // MOE_TM + N_EXPERTS
MOE_ROWS = MOE_BLOCKS * MOE_TM
SLOT_BITS = 16
assert MOE_TK + MOE_TM <= 1 << SLOT_BITS and N_TOK << SLOT_BITS < 1 << 31
DMA_UNROLL = 8


def _moe_kernel(be_ref, nv_ref, idx_ref, h_hbm, w1_ref, w3_ref, w2_ref, y_hbm, xf_s, xb_s, acc_s, sem):
    i = pl.program_id(0)
    j = pl.program_id(1)
    base = i * MOE_TM

    @pl.when(jnp.logical_and(i == 0, j == 0))
    def _():
        acc_s[...] = jnp.zeros_like(acc_s)
        spill = pltpu.make_async_copy(acc_s, y_hbm.at[pl.ds(MOE_TK, MOE_TM), :], sem.at[1])
        spill.start()
        spill.wait()

    @pl.when(i < nv_ref[0])
    def _():
        @pl.when(j == 0)
        def _():
            def gather(r, carry):
                tok = lax.shift_right_logical(idx_ref[base + r], SLOT_BITS)
                pltpu.make_async_copy(h_hbm.at[pl.ds(tok, 1), :], xf_s.at[pl.ds(r, 1), :], sem.at[0]).start()
                return carry

            lax.fori_loop(0, MOE_TM, gather, 0, unroll=DMA_UNROLL)
            pltpu.make_async_copy(h_hbm.at[pl.ds(0, MOE_TM), :], xf_s, sem.at[0]).wait()
            xb_s[...] = xf_s[...].astype(bf16)
            acc_s[...] = jnp.zeros_like(acc_s)

        acc_s[...] += _swiglu_step(xb_s[...], w1_ref, w3_ref, w2_ref)

        @pl.when(j == pl.num_programs(1) - 1)
        def _():
            def scatter(r, carry):
                slot = idx_ref[base + r] & ((1 << SLOT_BITS) - 1)
                pltpu.make_async_copy(acc_s.at[pl.ds(r, 1), :], y_hbm.at[pl.ds(slot, 1), :], sem.at[1]).start()
                return carry

            lax.fori_loop(0, MOE_TM, scatter, 0, unroll=DMA_UNROLL)
            pltpu.make_async_copy(acc_s, y_hbm.at[pl.ds(0, MOE_TM), :], sem.at[1]).wait()


def _moe_experts(row_idx, block_e, n_valid, h, w1, w3, w2, n):
    D = D_MODEL
    nf = D_FF // FFN_TF

    def jeff(i, j, nv):
        return jnp.where(i < nv[0], j, nf - 1)

    return pl.pallas_call(
        _moe_kernel,
        grid_spec=pltpu.PrefetchScalarGridSpec(
            num_scalar_prefetch=3,
            grid=(MOE_BLOCKS, nf),
            in_specs=[
                pl.BlockSpec(memory_space=pl.ANY),
                pl.BlockSpec((None, None, D, FFN_TF), lambda i, j, be, nv, ix: (n, be[i], 0, jeff(i, j, nv))),
                pl.BlockSpec((None, None, D, FFN_TF), lambda i, j, be, nv, ix: (n, be[i], 0, jeff(i, j, nv))),
                pl.BlockSpec((None, None, FFN_TF, D), lambda i, j, be, nv, ix: (n, be[i], jeff(i, j, nv), 0)),
            ],
            out_specs=pl.BlockSpec(memory_space=pl.ANY),
            scratch_shapes=[pltpu.VMEM((MOE_TM, D), f32), pltpu.VMEM((MOE_TM, D), bf16),
                            pltpu.VMEM((MOE_TM, D), f32), pltpu.SemaphoreType.DMA((2,))],
        ),
        out_shape=jax.ShapeDtypeStruct((MOE_TK + MOE_TM, D), f32),
        compiler_params=_params(2, disable_bounds_checks=True),
        name="moe_swiglu",
    )(block_e, n_valid, row_idx, h, w1, w3, w2)


ROUTE_TM = 1024
NEG_BIG = -1e30


def _router_kernel(h_ref, whi_ref, wlo_ref, b_ref, e_ref, g_ref):
    h = h_ref[...]
    h_hi = h.astype(bf16)
    h_lo = (h - h_hi.astype(f32)).astype(bf16)
    logits = (jnp.dot(h_hi, whi_ref[...], preferred_element_type=f32)
              + (jnp.dot(h_lo, whi_ref[...], preferred_element_type=f32)
                 + jnp.dot(h_hi, wlo_ref[...], preferred_element_type=f32))) + b_ref[...]
    lane = lax.broadcasted_iota(jnp.int32, logits.shape, 1)
    m1 = jnp.max(logits, axis=-1, keepdims=True)
    i1 = jnp.min(jnp.where(logits == m1, lane, LANE), axis=-1, keepdims=True)
    rest = jnp.where(lane == i1, -jnp.inf, logits)
    m2 = jnp.max(rest, axis=-1, keepdims=True)
    i2 = jnp.min(jnp.where(rest == m2, lane, LANE), axis=-1, keepdims=True)
    e2 = jnp.exp(m2 - m1)
    g1 = 1.0 / (1.0 + e2)
    e_ref[...] = jnp.where(lane == 0, i1, jnp.where(lane == 1, i2, 0))
    g_ref[...] = jnp.where(lane == 0, g1, jnp.where(lane == 1, e2 * g1, 0.0))


def _router(h, router_w, router_b):
    w = jnp.pad(router_w, ((0, 0), (0, LANE - N_EXPERTS)))
    w_hi = w.astype(bf16)
    w_lo = (w - w_hi.astype(f32)).astype(bf16)
    b = jnp.pad(router_b, (0, LANE - N_EXPERTS), constant_values=NEG_BIG)[None, :]
    tm = ROUTE_TM
    return pl.pallas_call(
        _router_kernel,
        grid=(N_TOK // tm,),
        in_specs=[pl.BlockSpec((tm, D_MODEL), lambda i: (i, 0)),
                  pl.BlockSpec((D_MODEL, LANE), lambda i: (0, 0)),
                  pl.BlockSpec((D_MODEL, LANE), lambda i: (0, 0)),
                  pl.BlockSpec((1, LANE), lambda i: (0, 0))],
        out_specs=[pl.BlockSpec((tm, LANE), lambda i: (i, 0)), pl.BlockSpec((tm, LANE), lambda i: (i, 0))],
        out_shape=[jax.ShapeDtypeStruct((N_TOK, LANE), jnp.int32), jax.ShapeDtypeStruct((N_TOK, LANE), f32)],
        compiler_params=_params(1),
        name="moe_router",
    )(h, w_hi, w_lo, b)


def _moe_plan(top_e):
    flat_e = top_e.reshape(-1)
    onehot = (flat_e[:, None] == jnp.arange(N_EXPERTS, dtype=jnp.int32)[None, :]).astype(jnp.int32)
    rank = jnp.cumsum(onehot, axis=0) - onehot
    counts = jnp.sum(onehot, axis=0)
    padded = (counts + MOE_TM - 1) // MOE_TM * MOE_TM
    pad_ends = jnp.cumsum(padded)
    pad_starts = pad_ends - padded
    dest = jnp.sum(onehot * (pad_starts[None, :] + rank), axis=1)
    pair = jnp.arange(MOE_TK, dtype=jnp.int32)
    packed = ((pair // TOP_K) << SLOT_BITS) | ((pair % TOP_K) * N_TOK + pair // TOP_K)
    spill = MOE_TK + jnp.arange(MOE_ROWS, dtype=jnp.int32) % MOE_TM
    row_idx = spill.at[dest].set(packed)
    n_valid = (pad_ends[-1] // MOE_TM).astype(jnp.int32)
    blocks = jnp.arange(MOE_BLOCKS, dtype=jnp.int32)
    block_e = jnp.sum((blocks[:, None] * MOE_TM >= pad_ends[None, :]).astype(jnp.int32), axis=1)
    block_e = jnp.minimum(block_e, N_EXPERTS - 1)
    last_e = block_e[jnp.maximum(n_valid - 1, 0)]
    block_e = jnp.where(blocks < n_valid, block_e, last_e)
    return row_idx, block_e, n_valid.reshape(1)


def _moe(h, router_w, router_b, w1, w3, w2, n):
    top_e, gates = _router(h, router_w, router_b)
    row_idx, block_e, n_valid = _moe_plan(top_e[:, :TOP_K])
    return _moe_experts(row_idx, block_e, n_valid, h, w1, w3, w2, n), gates


RES_TM = 512


def _resid_kernel(x_ref, *refs, n_f, emit_next):
    f_refs, refs = refs[:n_f], refs[n_f:]
    if n_f == 1:
        f = f_refs[0][...]
    else:
        gw_ref, refs = refs[0], refs[1:]
        f = gw_ref[:, 0:1] * f_refs[0][...]
        for k in range(1, n_f):
            f = f + gw_ref[:, k:k + 1] * f_refs[k][...]
    gate_ref, g_ref = refs[:2]
    x2 = x_ref[...] + gate_ref[...] * (_rms(f) * g_ref[...])
    if emit_next:
        sh_ref, sc_ref, gpre_ref, o_ref, h_ref = refs[2:]
        h_ref[...] = _modulated_norm(x2, gpre_ref, sc_ref, sh_ref).astype(bf16)
    else:
        o_ref, = refs[2:]
    o_ref[...] = x2


def _resid(x, f, mod, g_post, g_pre1, layer, gates=None):
    tm = RES_TM
    row = lambda i: (i, 0)
    tile = pl.BlockSpec((tm, D_MODEL), row)
    emit_next = layer + 1 < DEPTH
    if gates is None:
        n_f, args, in_specs = 1, [x, f], [tile, tile]
    else:
        n_f, args = TOP_K, [x] + [f] * TOP_K + [gates]
        choice = [pl.BlockSpec((tm, D_MODEL), functools.partial(lambda k, i: (k * (N_TOK // tm) + i, 0), k))
                  for k in range(TOP_K)]
        in_specs = [tile] + choice + [pl.BlockSpec((tm, LANE), row)]
    args += [mod, g_post]
    in_specs += [_mod_spec(layer, tm, 5), _vec_spec(layer, D_MODEL)]
    out_specs, out_shape = [tile], [jax.ShapeDtypeStruct((N_TOK, D_MODEL), f32)]
    if emit_next:
        args += [mod, mod, g_pre1]
        in_specs += [_mod_spec(layer + 1, tm, 0), _mod_spec(layer + 1, tm, 1), _vec_spec(layer + 1, D_MODEL)]
        out_specs.append(tile)
        out_shape.append(jax.ShapeDtypeStruct((N_TOK, D_MODEL), bf16))
    out = pl.pallas_call(
        functools.partial(_resid_kernel, n_f=n_f, emit_next=emit_next),
        grid=(N_TOK // tm,),
        in_specs=in_specs,
        out_specs=out_specs,
        out_shape=out_shape,
        compiler_params=_params(1),
        name="ffn_residual",
    )(*args)
    return (out[0], out[1]) if emit_next else (out[0], None)


def _row_params(p, width=None):
    p = p.reshape(DEPTH, -1)
    if width is not None and p.shape[1] < width:
        p = jnp.pad(p, ((0, 0), (0, width - p.shape[1])))
    return p[:, None, :]


def _mix_layer(x, h, mod, layer, prm, cache_k, cache_v, state_ssm, h2_dtype=bf16):
    proj, dt_raw = _inproj(h, prm['w_in'], prm['w_dt'], layer)
    y_pool = _pool(proj, prm['pool_w'], prm['pool_scale'], layer)
    act = _conv(proj, prm['conv_w'], prm['conv_b'], layer)
    sink = prm['attn_sink'][layer]
    o_ctx = _ctx_attention(proj, sink)
    o_lat = _lat_attention(proj, cache_k, cache_v, sink, layer)
    yf_c, yb_c, h_ctx = _ssd(act, dt_raw, None, prm['dt_bias'], prm['a_log'], layer,
                             n_seq=BATCH, seq_len=SEQ, row0=0, emit_state=True)
    yf_l, yb_l = _ssd(act, dt_raw, state_ssm, prm['dt_bias'], prm['a_log'], layer,
                      n_seq=DEC_BATCH, seq_len=DEC_SEQ, row0=N_CTX_TOK, emit_state=False)
    x1, h2 = _outproj(y_pool, o_ctx, o_lat, yf_c, yf_l, yb_c, yb_l, act, proj, x, mod, prm['d_skip'],
                      prm['g_ssm'], prm['g_post1'], prm['g_pre2'], prm['w_out'], layer, h2_dtype)
    k_ctx = proj[:N_CTX_TOK, COL_K:COL_K + KV_WIDTH].reshape(BATCH, SEQ, N_KV_HEADS, HEAD_DIM)
    v_ctx = proj[:N_CTX_TOK, COL_V:COL_V + KV_WIDTH].reshape(BATCH, SEQ, N_KV_HEADS, HEAD_DIM)
    h_ctx = h_ctx.reshape(BATCH, 2, SSM_HEADS, SSM_HEADDIM, SSM_STATE)
    return x1, h2, k_ctx, v_ctx, h_ctx


def _prepare(norm_mix_pre, norm_mix_post, norm_ffn_pre, norm_ffn_post, w_in, w_out, pool_w, pool_scale,
             attn_sink, conv_w, conv_b, dt_bias, a_log, ssm_d, ssm_norm):
    w_dt = jnp.pad(w_in[:, :, PROJ_W:], ((0, 0), (0, 0), (0, LANE - DT_W)))
    return dict(
        g_pre1=_row_params(norm_mix_pre), g_post1=_row_params(norm_mix_post),
        g_pre2=_row_params(norm_ffn_pre), g_post2=_row_params(norm_ffn_post),
        w_in=w_in[:, :, :PROJ_W].astype(bf16), w_dt=w_dt.astype(bf16), w_out=w_out.astype(bf16),
        pool_w=pool_w, pool_scale=_row_params(pool_scale), attn_sink=attn_sink,
        conv_w=conv_w, conv_b=_row_params(conv_b),
        dt_bias=_row_params(dt_bias, LANE), a_log=_row_params(a_log, LANE),
        d_skip=_row_params(jnp.repeat(ssm_d, SSM_HEADDIM, axis=1)), g_ssm=_row_params(ssm_norm),
    )


def kernel(x_prompt, x_sample, cache_k, cache_v, state_ssm, c, c_ctx, w_ada, b_ada, norm_mix_pre,
           norm_mix_post, norm_ffn_pre, norm_ffn_post, w_in, w_out, pool_w, pool_scale, attn_sink,
           conv_w, conv_b, dt_bias, a_log, ssm_d, ssm_norm, ffn_w1, ffn_w3, ffn_w2, router_w, router_b,
           moe_w1, moe_w3, moe_w2):
    prm = _prepare(norm_mix_pre, norm_mix_post, norm_ffn_pre, norm_ffn_post, w_in, w_out, pool_w, pool_scale,
                   attn_sink, conv_w, conv_b, dt_bias, a_log, ssm_d, ssm_norm)
    cond = jnp.concatenate([c_ctx[None, :], c, jnp.zeros((COND_PAD - N_COND, D_MODEL), f32)], axis=0)
    mod = _ada(cond, w_ada, b_ada)
    x = jnp.concatenate([x_prompt.reshape(N_CTX_TOK, D_MODEL), x_sample.reshape(N_LAT_TOK, D_MODEL)], axis=0)
    h0 = state_ssm.reshape(DEC_BATCH, DEPTH, 2, SSD_XW, SSM_STATE)
    ks, vs, hs = [], [], []
    h = _prenorm(x, mod, prm['g_pre1'], 0)
    for l in range(DEPTH):
        dense = l % 2 == 0
        x1, h2, k_l, v_l, h_l = _mix_layer(x, h, mod, l, prm, cache_k, cache_v, h0, bf16 if dense else f32)
        ks.append(k_l)
        vs.append(v_l)
        hs.append(h_l)
        i = l // 2
        if dense:
            f = _ffn(h2, ffn_w1, ffn_w3, ffn_w2, i)
            x, h = _resid(x1, f, mod, prm['g_post2'], prm['g_pre1'], l)
        else:
            y, gates = _moe(h2, router_w[i], router_b[i], moe_w1, moe_w3, moe_w2, i)
            x, h = _resid(x1, y, mod, prm['g_post2'], prm['g_pre1'], l, gates)
    y_p = x[:N_CTX_TOK].reshape(BATCH, SEQ, D_MODEL)
    y_s = x[N_CTX_TOK:].reshape(DEC_BATCH, DEC_SEQ, D_MODEL)
    return (y_p, y_s, jnp.stack(ks, axis=1), jnp.stack(vs, axis=1), jnp.stack(hs, axis=1))
```

```python
import functools

import numpy as np
import jax
import jax.numpy as jnp
from jax import lax
from jax.experimental import pallas as pl
from jax.experimental.pallas import tpu as pltpu

D_MODEL = 2048
BATCH = 16
SEQ = 256
DEPTH = 4
DEC_BATCH = 4
DEC_SEQ = 4096
PAST_LEN = 512
GRID_W = 64
POOL_WIDTH = D_MODEL // 4
ATTN_WIDTH = D_MODEL // 2
SSM_INNER = D_MODEL // 4
POOL_WINDOWS = (2, 4, 8, 16)
POOL_GROUPS = len(POOL_WINDOWS)
POOL_CH = POOL_WIDTH // POOL_GROUPS
HEAD_DIM = 128
N_HEADS = ATTN_WIDTH // HEAD_DIM
N_KV_HEADS = 2
GQA = N_HEADS // N_KV_HEADS
KV_WIDTH = N_KV_HEADS * HEAD_DIM
ATTN_WINDOW = 128
ATTN_BLOCK = 128
ATTN_SPAN = ATTN_BLOCK + 2 * ATTN_WINDOW
ATTN_SCALE = HEAD_DIM ** -0.5
ROPE_PAIRS = HEAD_DIM // 4
ROPE_BASE = 10000.0
SSM_HEADDIM = 64
SSM_HEADS = SSM_INNER // SSM_HEADDIM
SSM_GROUPS = 2
SSM_STATE = 128
SSM_CONV = 5
SSM_CHUNK = 128
CONV_DIM = SSM_INNER + 2 * SSM_GROUPS * SSM_STATE
D_FF = 7168
N_EXPERTS = 8
TOP_K = 2
RMS_EPS = 1e-6

N_CTX_TOK = BATCH * SEQ
N_LAT_TOK = DEC_BATCH * DEC_SEQ
N_TOK = N_CTX_TOK + N_LAT_TOK
N_COND = 1 + DEC_BATCH
COND_PAD = 8
assert N_CTX_TOK == DEC_SEQ

COL_U = 0
COL_Q = COL_U + POOL_WIDTH
COL_K = COL_Q + ATTN_WIDTH
COL_V = COL_K + KV_WIDTH
COL_Z = COL_V + KV_WIDTH
COL_XBC = COL_Z + SSM_INNER
PROJ_W = COL_XBC + CONV_DIM
DT_W = 2 * SSM_HEADS
LANE = 128
HALO = 8

V7X_VMEM_LIMIT = 56 * 1024 * 1024

FFN_TM = 1024
FFN_TF = 256
MOE_TM = 1024
PROJ_TM = 1024
PROJ_TN = 512
OUT_TM = 512
SEQ_TM = SEQ
ADA_TN = 1024

f32 = jnp.float32
bf16 = jnp.bfloat16


def _params(n_axes, **kw):
    return pltpu.CompilerParams(dimension_semantics=("arbitrary",) * n_axes,
                                vmem_limit_bytes=V7X_VMEM_LIMIT, **kw)


def _silu(x):
    return x * jax.nn.sigmoid(x)


def _rms(x):
    return x * lax.rsqrt(jnp.mean(x * x, axis=-1, keepdims=True) + RMS_EPS)


def _cond_row(tm):
    return lambda i: (i * tm) // DEC_SEQ


def _mod_spec(layer, tm, chunk):
    row = _cond_row(tm)
    return pl.BlockSpec((None, None, 1, D_MODEL), lambda i, *_: (layer, row(i), 0, chunk))


def _vec_spec(layer, width=None, col=0):
    return pl.BlockSpec((None, 1, width), lambda *_: (layer, 0, col))


def _ada_kernel(c_ref, w_ref, b_ref, o_ref):
    s = _silu(c_ref[...]).astype(bf16)
    o_ref[...] = jnp.dot(s, w_ref[...].astype(bf16), preferred_element_type=f32) + b_ref[...]


def _ada(cond, w_ada, b_ada):
    n = 6 * D_MODEL
    out = pl.pallas_call(
        _ada_kernel,
        grid=(DEPTH, n // ADA_TN),
        in_specs=[
            pl.BlockSpec((COND_PAD, D_MODEL), lambda l, j: (0, 0)),
            pl.BlockSpec((None, D_MODEL, ADA_TN), lambda l, j: (l, 0, j)),
            pl.BlockSpec((None, 1, ADA_TN), lambda l, j: (l, 0, j)),
        ],
        out_specs=pl.BlockSpec((None, COND_PAD, ADA_TN), lambda l, j: (l, 0, j)),
        out_shape=jax.ShapeDtypeStruct((DEPTH, COND_PAD, n), f32),
        compiler_params=_params(2),
        name="adaln_mod",
    )(cond, w_ada, b_ada.reshape(DEPTH, 1, n))
    return out.reshape(DEPTH, COND_PAD, 1, n)


def _modulated_norm(x, g_ref, sc_ref, sh_ref):
    return (_rms(x) * g_ref[...]) * (1.0 + sc_ref[...]) + sh_ref[...]


PRE_TM = 512


def _prenorm_kernel(x_ref, sh_ref, sc_ref, g_ref, h_ref):
    h_ref[...] = _modulated_norm(x_ref[...], g_ref, sc_ref, sh_ref).astype(bf16)


def _prenorm(x, mod, g_pre, layer):
    tm = PRE_TM
    row = lambda i: (i, 0)
    return pl.pallas_call(
        _prenorm_kernel,
        grid=(N_TOK // tm,),
        in_specs=[pl.BlockSpec((tm, D_MODEL), row), _mod_spec(layer, tm, 0), _mod_spec(layer, tm, 1),
                  _vec_spec(layer, D_MODEL)],
        out_specs=pl.BlockSpec((tm, D_MODEL), row),
        out_shape=jax.ShapeDtypeStruct((N_TOK, D_MODEL), bf16),
        compiler_params=_params(1),
        name="mix_prenorm",
    )(x, mod, mod, g_pre)


def _inproj_kernel(h_ref, w_ref, wdt_ref, o_ref, dt_ref):
    @pl.when(pl.program_id(1) == 0)
    def _():
        dt_ref[...] = jnp.dot(h_ref[...], wdt_ref[...], preferred_element_type=f32)

    o_ref[...] = jnp.dot(h_ref[...], w_ref[...], preferred_element_type=f32)


def _inproj(h, w_main, w_dt, layer):
    tm, tn = PROJ_TM, PROJ_TN
    return pl.pallas_call(
        _inproj_kernel,
        grid=(N_TOK // tm, PROJ_W // tn),
        in_specs=[
            pl.BlockSpec((tm, D_MODEL), lambda i, j: (i, 0)),
            pl.BlockSpec((None, D_MODEL, tn), lambda i, j: (layer, 0, j)),
            pl.BlockSpec((None, D_MODEL, LANE), lambda i, j: (layer, 0, 0)),
        ],
        out_specs=[
            pl.BlockSpec((tm, tn), lambda i, j: (i, j)),
            pl.BlockSpec((tm, LANE), lambda i, j: (i, 0)),
        ],
        out_shape=[jax.ShapeDtypeStruct((N_TOK, PROJ_W), f32),
                   jax.ShapeDtypeStruct((N_TOK, LANE), f32)],
        compiler_params=_params(2),
        name="in_proj",
    )(h, w_main, w_dt)


def _seq_tile_position(i):
    is_ctx = i < N_CTX_TOK // SEQ_TM
    t0 = jnp.where(is_ctx, 0, ((i - N_CTX_TOK // SEQ_TM) * SEQ_TM) % DEC_SEQ)
    seq_len = jnp.where(is_ctx, SEQ, DEC_SEQ)
    return t0, seq_len


def _fill_extended(e_s, cur_ref, prev_ref, next_ref, t0, seq_len):
    first = t0 == 0
    last = t0 + SEQ_TM == seq_len
    e_s[0:HALO, :] = jnp.where(first, 0.0, prev_ref[...])
    e_s[HALO:HALO + SEQ_TM, :] = cur_ref[...]
    e_s[HALO + SEQ_TM:, :] = jnp.where(last, 0.0, next_ref[...])


def _halo_specs(width, col):
    per = SEQ_TM // HALO
    n8 = N_TOK // HALO
    return [
        pl.BlockSpec((SEQ_TM, width), lambda i, *a: (i, col(*a))),
        pl.BlockSpec((HALO, width), lambda i, *a: (jnp.maximum(i * per - 1, 0), col(*a))),
        pl.BlockSpec((HALO, width), lambda i, *a: (jnp.minimum((i + 1) * per, n8 - 1), col(*a))),
    ]


def _pool_kernel(cur_ref, prev_ref, next_ref, w_ref, sc_ref, o_ref, e_s):
    t0, seq_len = _seq_tile_position(pl.program_id(0))
    _fill_extended(e_s, cur_ref, prev_ref, next_ref, t0, seq_len)
    t = t0 + lax.broadcasted_iota(jnp.int32, (SEQ_TM, 1), 0)
    for g, w in enumerate(POOL_WINDOWS):
        lanes = pl.ds(g * POOL_CH, POOL_CH)
        acc = e_s[pl.ds(HALO - w // 2, SEQ_TM), lanes]
        for s in range(1 - w // 2, w // 2):
            acc = acc + e_s[pl.ds(HALO + s, SEQ_TM), lanes]
        cnt = jnp.minimum(t + w // 2, seq_len) - jnp.maximum(t - w // 2, 0)
        mean = acc / cnt.astype(f32)
        d = (mean - e_s[pl.ds(HALO, SEQ_TM), lanes]).astype(bf16)
        y = jnp.dot(d, w_ref[g].astype(bf16), preferred_element_type=f32) * sc_ref[:, g * POOL_CH:(g + 1) * POOL_CH]
        o_ref[:, g * POOL_CH:(g + 1) * POOL_CH] = y.astype(bf16)


def _pool(proj, pool_w, pool_scale, layer):
    return pl.pallas_call(
        _pool_kernel,
        grid=(N_TOK // SEQ_TM,),
        in_specs=_halo_specs(POOL_WIDTH, lambda: COL_U // POOL_WIDTH) + [
            pl.BlockSpec((None, POOL_GROUPS, POOL_CH, POOL_CH), lambda i: (layer, 0, 0, 0)),
            _vec_spec(layer, POOL_WIDTH),
        ],
        out_specs=pl.BlockSpec((SEQ_TM, POOL_WIDTH), lambda i: (i, 0)),
        out_shape=jax.ShapeDtypeStruct((N_TOK, POOL_WIDTH), bf16),
        scratch_shapes=[pltpu.VMEM((SEQ_TM + 2 * HALO, POOL_WIDTH), f32)],
        compiler_params=_params(1),
        name="pool_mixer",
    )(proj, proj, proj, pool_w, pool_scale)


CONV_TN = 512


def _conv_kernel(cur_ref, prev_ref, next_ref, w_ref, b_ref, o_ref, e_s):
    t0, seq_len = _seq_tile_position(pl.program_id(0))
    _fill_extended(e_s, cur_ref, prev_ref, next_ref, t0, seq_len)
    acc = b_ref[...] + e_s[pl.ds(HALO - SSM_CONV // 2, SEQ_TM), :] * w_ref[0:1, :]
    for k in range(1, SSM_CONV):
        acc = acc + e_s[pl.ds(HALO + k - SSM_CONV // 2, SEQ_TM), :] * w_ref[k:k + 1, :]
    o_ref[...] = _silu(acc)


def _conv(proj, conv_w, conv_b, layer):
    col0 = COL_XBC // CONV_TN
    return pl.pallas_call(
        _conv_kernel,
        grid=(N_TOK // SEQ_TM, CONV_DIM // CONV_TN),
        in_specs=_halo_specs(CONV_TN, lambda j: col0 + j) + [
            pl.BlockSpec((None, SSM_CONV, CONV_TN), lambda i, j: (layer, 0, j)),
            pl.BlockSpec((None, 1, CONV_TN), lambda i, j: (layer, 0, j)),
        ],
        out_specs=pl.BlockSpec((SEQ_TM, CONV_TN), lambda i, j: (i, j)),
        out_shape=jax.ShapeDtypeStruct((N_TOK, CONV_DIM), f32),
        scratch_shapes=[pltpu.VMEM((SEQ_TM + 2 * HALO, CONV_TN), f32)],
        compiler_params=_params(2),
        name="ssm_conv",
    )(proj, proj, proj, conv_w, conv_b)


def _softmax_sink(s, sink):
    m = jnp.maximum(jnp.max(s, axis=-1, keepdims=True), sink)
    e = jnp.exp(s - m)
    return e / (jnp.sum(e, axis=-1, keepdims=True) + jnp.exp(sink - m))


def _qk(q, k):
    return lax.dot_general(q, k, (((1,), (1,)), ((), ())), preferred_element_type=f32)


def _ctx_attn_kernel(sink_ref, q_ref, k_ref, v_ref, o_ref):
    kvh = pl.program_id(1)
    k = k_ref[...].astype(bf16)
    v = v_ref[...].astype(bf16)
    for g in range(GQA):
        q = q_ref[:, g * HEAD_DIM:(g + 1) * HEAD_DIM].astype(bf16)
        p = _softmax_sink(_qk(q, k) * ATTN_SCALE, sink_ref[kvh * GQA + g])
        o = jnp.dot(p.astype(bf16), v, preferred_element_type=f32)
        o_ref[:, g * HEAD_DIM:(g + 1) * HEAD_DIM] = o.astype(bf16)


def _ctx_attention(proj, sink):
    qw = GQA * HEAD_DIM
    return pl.pallas_call(
        _ctx_attn_kernel,
        grid=(BATCH, N_KV_HEADS),
        in_specs=[
            pl.BlockSpec(memory_space=pltpu.SMEM),
            pl.BlockSpec((SEQ, qw), lambda b, h: (b, COL_Q // qw + h)),
            pl.BlockSpec((SEQ, HEAD_DIM), lambda b, h: (b, COL_K // HEAD_DIM + h)),
            pl.BlockSpec((SEQ, HEAD_DIM), lambda b, h: (b, COL_V // HEAD_DIM + h)),
        ],
        out_specs=pl.BlockSpec((SEQ, qw), lambda b, h: (b, h)),
        out_shape=jax.ShapeDtypeStruct((N_CTX_TOK, ATTN_WIDTH), bf16),
        compiler_params=_params(2),
        name="ctx_attention",
    )(sink, proj, proj, proj)


def _rope_tables():
    t = np.arange(DEC_SEQ)
    inv = ROPE_BASE ** (-np.arange(ROPE_PAIRS, dtype=np.float64) / ROPE_PAIRS)
    ang_r = (t // GRID_W)[:, None] * inv
    ang_c = (t % GRID_W)[:, None] * inv
    cos = np.concatenate([np.cos(ang_r), np.cos(ang_r), np.cos(ang_c), np.cos(ang_c)], axis=1)
    sin = np.concatenate([-np.sin(ang_r), np.sin(ang_r), -np.sin(ang_c), np.sin(ang_c)], axis=1)
    return jnp.asarray(cos, f32), jnp.asarray(sin, f32)


def _rope(x, cos, sin):
    lane = lax.broadcasted_iota(jnp.int32, x.shape, 1)
    partner = jnp.where(lane % (2 * ROPE_PAIRS) < ROPE_PAIRS,
                        pltpu.roll(x, HEAD_DIM - ROPE_PAIRS, 1), pltpu.roll(x, ROPE_PAIRS, 1))
    return x * cos + partner * sin


ROPE_ROWS = 512


def _lat_attn_kernel(sink_ref, q0_ref, q1_ref, k_ref, v_ref, ck_ref, cv_ref, cosq_ref, sinq_ref,
                     cos_ref, sin_ref, o_ref, k_s, v_s, ck_s, cv_s):
    qb = pl.program_id(1)

    @pl.when(qb == 0)
    def _():
        zeros = jnp.zeros((ATTN_WINDOW, KV_WIDTH), bf16)
        k_s[0:ATTN_WINDOW, :] = zeros
        v_s[0:ATTN_WINDOW, :] = zeros
        k_s[ATTN_WINDOW + DEC_SEQ:, :] = zeros
        v_s[ATTN_WINDOW + DEC_SEQ:, :] = zeros

        def stage(c, carry):
            r = pl.multiple_of(c * ROPE_ROWS, ROPE_ROWS)
            cos = cos_ref[pl.ds(r, ROPE_ROWS), :]
            sin = sin_ref[pl.ds(r, ROPE_ROWS), :]
            for h in range(N_KV_HEADS):
                kh = k_ref[pl.ds(r, ROPE_ROWS), h * HEAD_DIM:(h + 1) * HEAD_DIM]
                k_s[pl.ds(ATTN_WINDOW + r, ROPE_ROWS), h * HEAD_DIM:(h + 1) * HEAD_DIM] = _rope(kh, cos, sin).astype(bf16)
            v_s[pl.ds(ATTN_WINDOW + r, ROPE_ROWS), :] = v_ref[pl.ds(r, ROPE_ROWS), :].astype(bf16)
            return carry

        lax.fori_loop(0, DEC_SEQ // ROPE_ROWS, stage, 0)
        ck_s[...] = ck_ref[...].astype(bf16)
        cv_s[...] = cv_ref[...].astype(bf16)

    start = pl.multiple_of(qb * ATTN_BLOCK, ATTN_BLOCK)
    rows = GQA * ATTN_BLOCK
    a = lax.broadcasted_iota(jnp.int32, (rows, ATTN_SPAN), 0) % ATTN_BLOCK
    b = lax.broadcasted_iota(jnp.int32, (rows, ATTN_SPAN), 1)
    pos = start - ATTN_WINDOW + b
    mask = (b >= a) & (b <= a + 2 * ATTN_WINDOW) & (pos >= 0) & (pos < DEC_SEQ)
    head_of_row = lax.broadcasted_iota(jnp.int32, (rows, 1), 0) // ATTN_BLOCK
    cos = cosq_ref[...]
    sin = sinq_ref[...]
    for h, q_ref in enumerate((q0_ref, q1_ref)):
        q = jnp.concatenate(
            [_rope(q_ref[:, g * HEAD_DIM:(g + 1) * HEAD_DIM], cos, sin) for g in range(GQA)], axis=0).astype(bf16)
        sink = jnp.zeros((rows, 1), f32)
        for g in range(GQA):
            sink = jnp.where(head_of_row == g, sink_ref[h * GQA + g], sink)
        lanes = pl.ds(h * HEAD_DIM, HEAD_DIM)
        kw = k_s[pl.ds(start, ATTN_SPAN), lanes]
        vw = v_s[pl.ds(start, ATTN_SPAN), lanes]
        s_win = jnp.where(mask, _qk(q, kw) * ATTN_SCALE, -jnp.inf)
        s_ctx = _qk(q, ck_s[:, lanes]) * ATTN_SCALE
        m = jnp.maximum(jnp.maximum(jnp.max(s_win, axis=-1, keepdims=True),
                                    jnp.max(s_ctx, axis=-1, keepdims=True)), sink)
        e_win = jnp.exp(s_win - m)
        e_ctx = jnp.exp(s_ctx - m)
        den = (jnp.sum(e_win, axis=-1, keepdims=True) + jnp.sum(e_ctx, axis=-1, keepdims=True)
               + jnp.exp(sink - m))
        o = (jnp.dot((e_win / den).astype(bf16), vw, preferred_element_type=f32)
             + jnp.dot((e_ctx / den).astype(bf16), cv_s[:, lanes], preferred_element_type=f32))
        for g in range(GQA):
            col = (h * GQA + g) * HEAD_DIM
            o_ref[:, col:col + HEAD_DIM] = o[g * ATTN_BLOCK:(g + 1) * ATTN_BLOCK].astype(bf16)


def _lat_attention(proj, cache_k, cache_v, sink, layer):
    cos, sin = _rope_tables()
    qw = GQA * HEAD_DIM
    nqb = DEC_SEQ // ATTN_BLOCK
    row0 = N_CTX_TOK // ATTN_BLOCK
    seq0 = N_CTX_TOK // DEC_SEQ
    ck = cache_k.reshape(DEC_BATCH, DEPTH, PAST_LEN, KV_WIDTH)
    cv = cache_v.reshape(DEC_BATCH, DEPTH, PAST_LEN, KV_WIDTH)
    ctx_spec = pl.BlockSpec((None, None, PAST_LEN, KV_WIDTH), lambda b, n: (b, layer, 0, 0))
    return pl.pallas_call(
        _lat_attn_kernel,
        grid=(DEC_BATCH, nqb),
        in_specs=[
            pl.BlockSpec(memory_space=pltpu.SMEM),
            pl.BlockSpec((ATTN_BLOCK, qw), lambda b, n: (row0 + b * nqb + n, COL_Q // qw)),
            pl.BlockSpec((ATTN_BLOCK, qw), lambda b, n: (row0 + b * nqb + n, COL_Q // qw + 1)),
            pl.BlockSpec((DEC_SEQ, KV_WIDTH), lambda b, n: (seq0 + b, COL_K // KV_WIDTH)),
            pl.BlockSpec((DEC_SEQ, KV_WIDTH), lambda b, n: (seq0 + b, COL_V // KV_WIDTH)),
            ctx_spec, ctx_spec,
            pl.BlockSpec((ATTN_BLOCK, HEAD_DIM), lambda b, n: (n, 0)),
            pl.BlockSpec((ATTN_BLOCK, HEAD_DIM), lambda b, n: (n, 0)),
            pl.BlockSpec((DEC_SEQ, HEAD_DIM), lambda b, n: (0, 0)),
            pl.BlockSpec((DEC_SEQ, HEAD_DIM), lambda b, n: (0, 0)),
        ],
        out_specs=pl.BlockSpec((ATTN_BLOCK, ATTN_WIDTH), lambda b, n: (b * nqb + n, 0)),
        out_shape=jax.ShapeDtypeStruct((N_LAT_TOK, ATTN_WIDTH), bf16),
        scratch_shapes=[
            pltpu.VMEM((DEC_SEQ + 2 * ATTN_WINDOW, KV_WIDTH), bf16),
            pltpu.VMEM((DEC_SEQ + 2 * ATTN_WINDOW, KV_WIDTH), bf16),
            pltpu.VMEM((PAST_LEN, KV_WIDTH), bf16),
            pltpu.VMEM((PAST_LEN, KV_WIDTH), bf16),
        ],
        compiler_params=_params(2),
        name="latent_attention",
    )(sink, proj, proj, proj, proj, ck, cv, cos, sin, cos, sin)


SSD_XW = SSM_HEADS * SSM_HEADDIM
SSD_LW = SSM_HEADS * LANE
PAIR_W = 2 * SSM_HEADDIM
GROUP_W = SSD_XW // SSM_GROUPS


def _split3(x):
    hi = x.astype(bf16)
    r = x - hi.astype(f32)
    mid = r.astype(bf16)
    lo = (r - mid.astype(f32)).astype(bf16)
    return hi, mid, lo


def _dot_exact_rhs(sel, x):
    return sum(jnp.dot(sel, p, preferred_element_type=f32) for p in _split3(x))


def _dot_exact_lhs(x, sel):
    return sum(jnp.dot(p, sel, preferred_element_type=f32) for p in _split3(x))


def _ssd_constants():
    q = SSM_CHUNK
    lower = np.tril(np.ones((q, q), np.float32))
    tri = np.stack([lower, lower.T])
    x_wide = np.zeros((2, LANE, SSD_LW), np.float32)
    x_head = np.zeros((2, LANE, SSD_XW), np.float32)
    for d in range(2):
        for j in range(SSM_HEADS):
            x_wide[d, d * SSM_HEADS + j, j * LANE:(j + 1) * LANE] = 1.0
            x_head[d, d * SSM_HEADS + j, j * SSM_HEADDIM:(j + 1) * SSM_HEADDIM] = 1.0
    return jnp.asarray(tri, bf16), jnp.asarray(x_wide, bf16), jnp.asarray(x_head, bf16)


def _ssd_direction(d, act, dt_raw, h_s, bias, a_neg, tri_ref, xw_ref, xh_ref):
    q = SSM_CHUNK
    dt = jnp.logaddexp(dt_raw + bias, 0.0)
    a = dt * a_neg
    cs = _dot_exact_rhs(tri_ref[d], a)
    cs_wide = _dot_exact_lhs(cs, xw_ref[d])
    cs_head = _dot_exact_lhs(cs, xh_ref[d])
    dt_head = _dot_exact_lhs(dt, xh_ref[d])
    cs_t = cs.T
    edge = cs_head[q - 1:q, :] if d == 0 else cs_head[0:1, :]
    xd = act[:, 0:SSD_XW] * dt_head
    xdb = xd.astype(bf16)
    xdw = (xd * jnp.exp(edge - cs_head)).astype(bf16)
    li = lax.broadcasted_iota(jnp.int32, (q, q), 0)
    si = lax.broadcasted_iota(jnp.int32, (q, q), 1)
    seen = (li >= si) if d == 0 else (li <= si)
    lane = lax.broadcasted_iota(jnp.int32, (q, PAIR_W), 1)
    h_in = h_s[...]
    ys = []
    for g in range(SSM_GROUPS):
        bm = act[:, SSD_XW + g * SSM_STATE:SSD_XW + (g + 1) * SSM_STATE]
        cm = act[:, SSD_XW + (SSM_GROUPS + g) * SSM_STATE:SSD_XW + (SSM_GROUPS + g + 1) * SSM_STATE]
        cmb = cm.astype(bf16)
        cb = _qk(cmb, bm.astype(bf16))
        cols = slice(g * GROUP_W, (g + 1) * GROUP_W)
        y_off = jnp.dot(cmb, h_in[:, cols].astype(bf16), preferred_element_type=f32) * jnp.exp(cs_head[:, cols])
        y_diag = []
        for pr in range(GROUP_W // PAIR_W):
            pcols = slice(g * GROUP_W + pr * PAIR_W, g * GROUP_W + (pr + 1) * PAIR_W)
            halves = []
            for jj in range(2):
                j = (g * GROUP_W + pr * PAIR_W) // SSM_HEADDIM + jj
                row = d * SSM_HEADS + j
                seg = cs_wide[:, j * LANE:(j + 1) * LANE] - cs_t[row:row + 1, :]
                m = (cb * jnp.exp(jnp.where(seen, seg, -jnp.inf))).astype(bf16)
                halves.append(jnp.dot(m, xdb[:, pcols], preferred_element_type=f32))
            y_diag.append(jnp.where(lane < SSM_HEADDIM, halves[0], halves[1]))
        ys.append(jnp.concatenate(y_diag, axis=1) + y_off)
        st = jnp.dot(bm.T.astype(bf16), xdw[:, cols], preferred_element_type=f32)
        h_s[:, cols] = h_in[:, cols] * jnp.exp(edge[:, cols]) + st
    return jnp.concatenate(ys, axis=1)


def _ssd_kernel(*refs, has_h0, emit_state):
    refs = list(refs)
    af_ref, ab_ref, dtf_ref, dtb_ref = refs[:4]
    refs = refs[4:]
    h0_ref = refs.pop(0) if has_h0 else None
    bias_ref, alog_ref, tri_ref, xw_ref, xh_ref = refs[:5]
    refs = refs[5:]
    yf_ref, yb_ref = refs[:2]
    refs = refs[2:]
    hout_ref = refs.pop(0) if emit_state else None
    hf_s, hb_s = refs
    c = pl.program_id(1)
    blocks = SSD_XW // LANE

    @pl.when(c == 0)
    def _():
        for d, h_s in enumerate((hf_s, hb_s)):
            if has_h0:
                for k in range(blocks):
                    h_s[:, k * LANE:(k + 1) * LANE] = h0_ref[d, k * LANE:(k + 1) * LANE, :].T
            else:
                h_s[...] = jnp.zeros_like(h_s)

    lane = lax.broadcasted_iota(jnp.int32, (1, LANE), 1)
    a_neg = jnp.where(lane < DT_W, -jnp.exp(alog_ref[...]), 0.0)
    bias = bias_ref[...]
    yf_ref[...] = _ssd_direction(0, af_ref[...], dtf_ref[...], hf_s, bias, a_neg, tri_ref, xw_ref, xh_ref)
    yb_ref[...] = _ssd_direction(1, ab_ref[...], dtb_ref[...], hb_s, bias, a_neg, tri_ref, xw_ref, xh_ref)

    if emit_state:
        @pl.when(c == pl.num_programs(1) - 1)
        def _():
            for d, h_s in enumerate((hf_s, hb_s)):
                for k in range(blocks):
                    hout_ref[d, k * LANE:(k + 1) * LANE, :] = h_s[:, k * LANE:(k + 1) * LANE].T


def _ssd(act, dt_raw, h0, dt_bias, a_log, layer, *, n_seq, seq_len, row0, emit_state):
    q = SSM_CHUNK
    nc = seq_len // q
    blk0 = row0 // q
    tri, x_wide, x_head = _ssd_constants()
    fwd = lambda s, c: (blk0 + s * nc + c, 0)
    bwd = lambda s, c: (blk0 + s * nc + nc - 1 - c, 0)
    const3 = lambda s, c: (0, 0, 0)
    in_specs = [
        pl.BlockSpec((q, CONV_DIM), fwd), pl.BlockSpec((q, CONV_DIM), bwd),
        pl.BlockSpec((q, LANE), fwd), pl.BlockSpec((q, LANE), bwd),
    ]
    args = [act, act, dt_raw, dt_raw]
    if h0 is not None:
        in_specs.append(pl.BlockSpec((None, None, 2, SSD_XW, SSM_STATE), lambda s, c: (s, layer, 0, 0, 0)))
        args.append(h0)
    in_specs += [
        _vec_spec(layer, LANE), _vec_spec(layer, LANE),
        pl.BlockSpec(tri.shape, const3), pl.BlockSpec(x_wide.shape, const3), pl.BlockSpec(x_head.shape, const3),
    ]
    args += [dt_bias, a_log, tri, x_wide, x_head]
    rows = n_seq * seq_len
    out_specs = [pl.BlockSpec((q, SSD_XW), lambda s, c: (s * nc + c, 0)),
                 pl.BlockSpec((q, SSD_XW), lambda s, c: (s * nc + nc - 1 - c, 0))]
    out_shape = [jax.ShapeDtypeStruct((rows, SSD_XW), f32)] * 2
    if emit_state:
        out_specs.append(pl.BlockSpec((None, 2, SSD_XW, SSM_STATE), lambda s, c: (s, 0, 0, 0)))
        out_shape.append(jax.ShapeDtypeStruct((n_seq, 2, SSD_XW, SSM_STATE), f32))
    return pl.pallas_call(
        functools.partial(_ssd_kernel, has_h0=h0 is not None, emit_state=emit_state),
        grid=(n_seq, nc),
        in_specs=in_specs,
        out_specs=out_specs,
        out_shape=out_shape,
        scratch_shapes=[pltpu.VMEM((SSM_STATE, SSD_XW), f32), pltpu.VMEM((SSM_STATE, SSD_XW), f32)],
        compiler_params=_params(2),
        name="ssd_scan",
    )(*args)


OUT_TN = 512
OUT_NT = D_MODEL // OUT_TN


def _outproj_kernel(yp_ref, oc_ref, ol_ref, yfc_ref, yfl_ref, ybc_ref, ybl_ref, xs_ref, z_ref, x_ref,
                    gate_ref, sh2_ref, sc2_ref, dx_ref, gssm_ref, gpost_ref, gpre2_ref, w_ref,
                    x1_ref, h2_ref, a_s, m_s):
    j = pl.program_id(1)

    @pl.when(j == 0)
    def _():
        is_ctx = pl.program_id(0) < N_CTX_TOK // OUT_TM
        yf = jnp.where(is_ctx, yfc_ref[...], yfl_ref[...])
        yb = jnp.where(is_ctx, ybc_ref[...], ybl_ref[...])
        y = (yf + yb + xs_ref[...] * dx_ref[...]) * _silu(z_ref[...])
        a_s[:, 0:POOL_WIDTH] = yp_ref[...]
        a_s[:, POOL_WIDTH:POOL_WIDTH + ATTN_WIDTH] = jnp.where(is_ctx, oc_ref[...], ol_ref[...])
        a_s[:, POOL_WIDTH + ATTN_WIDTH:] = (_rms(y) * gssm_ref[...]).astype(bf16)

    m_s[j] = jnp.dot(a_s[...], w_ref[...], preferred_element_type=f32)

    @pl.when(j == OUT_NT - 1)
    def _():
        ss = sum(jnp.sum(m_s[k] * m_s[k], axis=-1, keepdims=True) for k in range(OUT_NT))
        r = lax.rsqrt(ss / D_MODEL + RMS_EPS)
        ss1 = jnp.zeros_like(ss)
        for k in range(OUT_NT):
            c = slice(k * OUT_TN, (k + 1) * OUT_TN)
            x1 = x_ref[:, c] + gate_ref[:, c] * ((m_s[k] * r) * gpost_ref[:, c])
            x1_ref[:, c] = x1
            ss1 = ss1 + jnp.sum(x1 * x1, axis=-1, keepdims=True)
        r1 = lax.rsqrt(ss1 / D_MODEL + RMS_EPS)
        for k in range(OUT_NT):
            c = slice(k * OUT_TN, (k + 1) * OUT_TN)
            h2 = ((x1_ref[:, c] * r1) * gpre2_ref[:, c]) * (1.0 + sc2_ref[:, c]) + sh2_ref[:, c]
            h2_ref[:, c] = h2.astype(h2_ref.dtype)


def _outproj(y_pool, o_ctx, o_lat, yf_c, yf_l, yb_c, yb_l, act, proj, x, mod, d_skip, g_ssm, g_post, g_pre2,
             w_out, layer, h2_dtype):
    tm = OUT_TM
    n_ctx = N_CTX_TOK // tm
    row = lambda i, j: (i, 0)
    ctx = lambda i, j: (jnp.minimum(i, n_ctx - 1), 0)
    lat = lambda i, j: (jnp.maximum(i - n_ctx, 0), 0)
    return pl.pallas_call(
        _outproj_kernel,
        grid=(N_TOK // tm, OUT_NT),
        in_specs=[
            pl.BlockSpec((tm, POOL_WIDTH), row),
            pl.BlockSpec((tm, ATTN_WIDTH), ctx), pl.BlockSpec((tm, ATTN_WIDTH), lat),
            pl.BlockSpec((tm, SSM_INNER), ctx), pl.BlockSpec((tm, SSM_INNER), lat),
            pl.BlockSpec((tm, SSM_INNER), ctx), pl.BlockSpec((tm, SSM_INNER), lat),
            pl.BlockSpec((tm, SSM_INNER), row),
            pl.BlockSpec((tm, SSM_INNER), lambda i, j: (i, COL_Z // SSM_INNER)),
            pl.BlockSpec((tm, D_MODEL), row),
            _mod_spec(layer, tm, 2), _mod_spec(layer, tm, 3), _mod_spec(layer, tm, 4),
            _vec_spec(layer, SSM_INNER), _vec_spec(layer, SSM_INNER),
            _vec_spec(layer, D_MODEL), _vec_spec(layer, D_MODEL),
            pl.BlockSpec((None, D_MODEL, OUT_TN), lambda i, j: (layer, 0, j)),
        ],
        out_specs=[pl.BlockSpec((tm, D_MODEL), row), pl.BlockSpec((tm, D_MODEL), row)],
        out_shape=[jax.ShapeDtypeStruct((N_TOK, D_MODEL), f32), jax.ShapeDtypeStruct((N_TOK, D_MODEL), h2_dtype)],
        scratch_shapes=[pltpu.VMEM((tm, D_MODEL), bf16), pltpu.VMEM((OUT_NT, tm, OUT_TN), f32)],
        compiler_params=_params(2),
        name="out_proj",
    )(y_pool, o_ctx, o_lat, yf_c, yf_l, yb_c, yb_l, act, proj, x, mod, mod, mod, d_skip, g_ssm, g_post, g_pre2, w_out)


def _swiglu_step(h, w1_ref, w3_ref, w2_ref):
    a = jnp.dot(h, w1_ref[...].astype(bf16), preferred_element_type=f32)
    b = jnp.dot(h, w3_ref[...].astype(bf16), preferred_element_type=f32)
    g = (_silu(a) * b).astype(bf16)
    return jnp.dot(g, w2_ref[...].astype(bf16), preferred_element_type=f32)


def _ffn_kernel(h_ref, w1_ref, w3_ref, w2_ref, o_ref):
    @pl.when(pl.program_id(1) == 0)
    def _():
        o_ref[...] = jnp.zeros_like(o_ref)

    o_ref[...] += _swiglu_step(h_ref[...], w1_ref, w3_ref, w2_ref)


def _ffn(h, w1, w3, w2, n):
    T, D = h.shape
    F = w1.shape[2]
    return pl.pallas_call(
        _ffn_kernel,
        grid=(T // FFN_TM, F // FFN_TF),
        in_specs=[
            pl.BlockSpec((FFN_TM, D), lambda i, j: (i, 0)),
            pl.BlockSpec((None, D, FFN_TF), lambda i, j: (n, 0, j)),
            pl.BlockSpec((None, D, FFN_TF), lambda i, j: (n, 0, j)),
            pl.BlockSpec((None, FFN_TF, D), lambda i, j: (n, j, 0)),
        ],
        out_specs=pl.BlockSpec((FFN_TM, D), lambda i, j: (i, 0)),
        out_shape=jax.ShapeDtypeStruct((T, D), f32),
        compiler_params=_params(2),
        name="ffn_swiglu",
    )(h, w1, w3, w2)


MOE_TK = N_TOK * TOP_K
MOE_BLOCKS = MOE_TK // MOE_TM + N_EXPERTS
MOE_ROWS = MOE_BLOCKS * MOE_TM
SLOT_BITS = 16
assert MOE_TK + MOE_TM <= 1 << SLOT_BITS and N_TOK << SLOT_BITS < 1 << 31
DMA_UNROLL = 8
MOE_SUB = 256


def _moe_kernel(be_ref, nv_ref, nq_ref, idx_ref, h_hbm, w1_ref, w3_ref, w2_ref, y_hbm, xf_s, xb_s, acc_s, sem):
    i = pl.program_id(0)
    j = pl.program_id(1)
    base = i * MOE_TM

    @pl.when(jnp.logical_and(i == 0, j == 0))
    def _():
        acc_s[...] = jnp.zeros_like(acc_s)
        spill = pltpu.make_async_copy(acc_s, y_hbm.at[pl.ds(MOE_TK, MOE_TM), :], sem.at[1])
        spill.start()
        spill.wait()

    groups = nq_ref[i]

    @pl.when(groups > 0)
    def _():
        @pl.when(j == 0)
        def _():
            def gather(r, carry):
                tok = lax.shift_right_logical(idx_ref[base + r], SLOT_BITS)
                pltpu.make_async_copy(h_hbm.at[pl.ds(tok, 1), :], xf_s.at[pl.ds(r, 1), :], sem.at[0]).start()
                return carry

            lax.fori_loop(0, MOE_TM, gather, 0, unroll=DMA_UNROLL)
            pltpu.make_async_copy(h_hbm.at[pl.ds(0, MOE_TM), :], xf_s, sem.at[0]).wait()
            xb_s[...] = xf_s[...].astype(bf16)
            acc_s[...] = jnp.zeros_like(acc_s)

        for q in range(1, MOE_TM // MOE_SUB + 1):
            @pl.when(groups == q)
            def _(rows=q * MOE_SUB):
                acc_s[0:rows, :] += _swiglu_step(xb_s[0:rows, :], w1_ref, w3_ref, w2_ref)

        @pl.when(j == pl.num_programs(1) - 1)
        def _():
            def scatter(r, carry):
                slot = idx_ref[base + r] & ((1 << SLOT_BITS) - 1)
                pltpu.make_async_copy(acc_s.at[pl.ds(r, 1), :], y_hbm.at[pl.ds(slot, 1), :], sem.at[1]).start()
                return carry

            lax.fori_loop(0, MOE_TM, scatter, 0, unroll=DMA_UNROLL)
            pltpu.make_async_copy(acc_s, y_hbm.at[pl.ds(0, MOE_TM), :], sem.at[1]).wait()


def _moe_experts(row_idx, block_e, n_valid, groups, h, w1, w3, w2, n):
    D = D_MODEL
    nf = D_FF // FFN_TF

    def jeff(i, j, nv):
        return jnp.where(i < nv[0], j, nf - 1)

    return pl.pallas_call(
        _moe_kernel,
        grid_spec=pltpu.PrefetchScalarGridSpec(
            num_scalar_prefetch=4,
            grid=(MOE_BLOCKS, nf),
            in_specs=[
                pl.BlockSpec(memory_space=pl.ANY),
                pl.BlockSpec((None, None, D, FFN_TF), lambda i, j, be, nv, nq, ix: (n, be[i], 0, jeff(i, j, nv))),
                pl.BlockSpec((None, None, D, FFN_TF), lambda i, j, be, nv, nq, ix: (n, be[i], 0, jeff(i, j, nv))),
                pl.BlockSpec((None, None, FFN_TF, D), lambda i, j, be, nv, nq, ix: (n, be[i], jeff(i, j, nv), 0)),
            ],
            out_specs=pl.BlockSpec(memory_space=pl.ANY),
            scratch_shapes=[pltpu.VMEM((MOE_TM, D), f32), pltpu.VMEM((MOE_TM, D), bf16),
                            pltpu.VMEM((MOE_TM, D), f32), pltpu.SemaphoreType.DMA((2,))],
        ),
        out_shape=jax.ShapeDtypeStruct((MOE_TK + MOE_TM, D), f32),
        compiler_params=_params(2, disable_bounds_checks=True),
        name="moe_swiglu",
    )(block_e, n_valid, groups, row_idx, h, w1, w3, w2)


ROUTE_TM = 1024
NEG_BIG = -1e30


def _router_kernel(h_ref, whi_ref, wlo_ref, b_ref, e_ref, g_ref):
    h = h_ref[...]
    h_hi = h.astype(bf16)
    h_lo = (h - h_hi.astype(f32)).astype(bf16)
    logits = (jnp.dot(h_hi, whi_ref[...], preferred_element_type=f32)
              + (jnp.dot(h_lo, whi_ref[...], preferred_element_type=f32)
                 + jnp.dot(h_hi, wlo_ref[...], preferred_element_type=f32))) + b_ref[...]
    lane = lax.broadcasted_iota(jnp.int32, logits.shape, 1)
    m1 = jnp.max(logits, axis=-1, keepdims=True)
    i1 = jnp.min(jnp.where(logits == m1, lane, LANE), axis=-1, keepdims=True)
    rest = jnp.where(lane == i1, -jnp.inf, logits)
    m2 = jnp.max(rest, axis=-1, keepdims=True)
    i2 = jnp.min(jnp.where(rest == m2, lane, LANE), axis=-1, keepdims=True)
    e2 = jnp.exp(m2 - m1)
    g1 = 1.0 / (1.0 + e2)
    e_ref[...] = jnp.where(lane == 0, i1, jnp.where(lane == 1, i2, 0))
    g_ref[...] = jnp.where(lane == 0, g1, jnp.where(lane == 1, e2 * g1, 0.0))


def _router(h, router_w, router_b):
    w = jnp.pad(router_w, ((0, 0), (0, LANE - N_EXPERTS)))
    w_hi = w.astype(bf16)
    w_lo = (w - w_hi.astype(f32)).astype(bf16)
    b = jnp.pad(router_b, (0, LANE - N_EXPERTS), constant_values=NEG_BIG)[None, :]
    tm = ROUTE_TM
    return pl.pallas_call(
        _router_kernel,
        grid=(N_TOK // tm,),
        in_specs=[pl.BlockSpec((tm, D_MODEL), lambda i: (i, 0)),
                  pl.BlockSpec((D_MODEL, LANE), lambda i: (0, 0)),
                  pl.BlockSpec((D_MODEL, LANE), lambda i: (0, 0)),
                  pl.BlockSpec((1, LANE), lambda i: (0, 0))],
        out_specs=[pl.BlockSpec((tm, LANE), lambda i: (i, 0)), pl.BlockSpec((tm, LANE), lambda i: (i, 0))],
        out_shape=[jax.ShapeDtypeStruct((N_TOK, LANE), jnp.int32), jax.ShapeDtypeStruct((N_TOK, LANE), f32)],
        compiler_params=_params(1),
        name="moe_router",
    )(h, w_hi, w_lo, b)


def _moe_plan(top_e):
    flat_e = top_e.reshape(-1)
    onehot = (flat_e[:, None] == jnp.arange(N_EXPERTS, dtype=jnp.int32)[None, :]).astype(jnp.int32)
    rank = jnp.cumsum(onehot, axis=0) - onehot
    counts = jnp.sum(onehot, axis=0)
    padded = (counts + MOE_TM - 1) // MOE_TM * MOE_TM
    pad_ends = jnp.cumsum(padded)
    pad_starts = pad_ends - padded
    dest = jnp.sum(onehot * (pad_starts[None, :] + rank), axis=1)
    pair = jnp.arange(MOE_TK, dtype=jnp.int32)
    packed = ((pair // TOP_K) << SLOT_BITS) | ((pair % TOP_K) * N_TOK + pair // TOP_K)
    spill = MOE_TK + jnp.arange(MOE_ROWS, dtype=jnp.int32) % MOE_TM
    row_idx = spill.at[dest].set(packed, unique_indices=True)
    n_valid = (pad_ends[-1] // MOE_TM).astype(jnp.int32)
    blocks = jnp.arange(MOE_BLOCKS, dtype=jnp.int32)
    block_e = jnp.sum((blocks[:, None] * MOE_TM >= pad_ends[None, :]).astype(jnp.int32), axis=1)
    block_e = jnp.minimum(block_e, N_EXPERTS - 1)
    real_rows = jnp.clip((pad_starts + counts)[block_e] - blocks * MOE_TM, 0, MOE_TM)
    groups = jnp.where(blocks < n_valid, (real_rows + MOE_SUB - 1) // MOE_SUB, 0)
    last_e = block_e[jnp.maximum(n_valid - 1, 0)]
    block_e = jnp.where(blocks < n_valid, block_e, last_e)
    return row_idx, block_e, n_valid.reshape(1), groups


def _moe(h, router_w, router_b, w1, w3, w2, n):
    top_e, gates = _router(h, router_w, router_b)
    row_idx, block_e, n_valid, groups = _moe_plan(top_e[:, :TOP_K])
    return _moe_experts(row_idx, block_e, n_valid, groups, h, w1, w3, w2, n), gates


RES_TM = 512


def _resid_kernel(x_ref, *refs, n_f, emit_next):
    f_refs, refs = refs[:n_f], refs[n_f:]
    if n_f == 1:
        f = f_refs[0][...]
    else:
        gw_ref, refs = refs[0], refs[1:]
        f = gw_ref[:, 0:1] * f_refs[0][...]
        for k in range(1, n_f):
            f = f + gw_ref[:, k:k + 1] * f_refs[k][...]
    gate_ref, g_ref = refs[:2]
    x2 = x_ref[...] + gate_ref[...] * (_rms(f) * g_ref[...])
    if emit_next:
        sh_ref, sc_ref, gpre_ref, o_ref, h_ref = refs[2:]
        h_ref[...] = _modulated_norm(x2, gpre_ref, sc_ref, sh_ref).astype(bf16)
    else:
        o_ref, = refs[2:]
    o_ref[...] = x2


def _resid(x, f, mod, g_post, g_pre1, layer, gates=None):
    tm = RES_TM
    row = lambda i: (i, 0)
    tile = pl.BlockSpec((tm, D_MODEL), row)
    emit_next = layer + 1 < DEPTH
    if gates is None:
        n_f, args, in_specs = 1, [x, f], [tile, tile]
    else:
        n_f, args = TOP_K, [x] + [f] * TOP_K + [gates]
        choice = [pl.BlockSpec((tm, D_MODEL), functools.partial(lambda k, i: (k * (N_TOK // tm) + i, 0), k))
                  for k in range(TOP_K)]
        in_specs = [tile] + choice + [pl.BlockSpec((tm, LANE), row)]
    args += [mod, g_post]
    in_specs += [_mod_spec(layer, tm, 5), _vec_spec(layer, D_MODEL)]
    out_specs, out_shape = [tile], [jax.ShapeDtypeStruct((N_TOK, D_MODEL), f32)]
    if emit_next:
        args += [mod, mod, g_pre1]
        in_specs += [_mod_spec(layer + 1, tm, 0), _mod_spec(layer + 1, tm, 1), _vec_spec(layer + 1, D_MODEL)]
        out_specs.append(tile)
        out_shape.append(jax.ShapeDtypeStruct((N_TOK, D_MODEL), bf16))
    out = pl.pallas_call(
        functools.partial(_resid_kernel, n_f=n_f, emit_next=emit_next),
        grid=(N_TOK // tm,),
        in_specs=in_specs,
        out_specs=out_specs,
        out_shape=out_shape,
        compiler_params=_params(1),
        name="ffn_residual",
    )(*args)
    return (out[0], out[1]) if emit_next else (out[0], None)


def _row_params(p, width=None):
    p = p.reshape(DEPTH, -1)
    if width is not None and p.shape[1] < width:
        p = jnp.pad(p, ((0, 0), (0, width - p.shape[1])))
    return p[:, None, :]


def _mix_layer(x, h, mod, layer, prm, cache_k, cache_v, state_ssm, h2_dtype=bf16):
    proj, dt_raw = _inproj(h, prm['w_in'], prm['w_dt'], layer)
    y_pool = _pool(proj, prm['pool_w'], prm['pool_scale'], layer)
    act = _conv(proj, prm['conv_w'], prm['conv_b'], layer)
    sink = prm['attn_sink'][layer]
    o_ctx = _ctx_attention(proj, sink)
    o_lat = _lat_attention(proj, cache_k, cache_v, sink, layer)
    yf_c, yb_c, h_ctx = _ssd(act, dt_raw, None, prm['dt_bias'], prm['a_log'], layer,
                             n_seq=BATCH, seq_len=SEQ, row0=0, emit_state=True)
    yf_l, yb_l = _ssd(act, dt_raw, state_ssm, prm['dt_bias'], prm['a_log'], layer,
                      n_seq=DEC_BATCH, seq_len=DEC_SEQ, row0=N_CTX_TOK, emit_state=False)
    x1, h2 = _outproj(y_pool, o_ctx, o_lat, yf_c, yf_l, yb_c, yb_l, act, proj, x, mod, prm['d_skip'],
                      prm['g_ssm'], prm['g_post1'], prm['g_pre2'], prm['w_out'], layer, h2_dtype)
    k_ctx = proj[:N_CTX_TOK, COL_K:COL_K + KV_WIDTH].reshape(BATCH, SEQ, N_KV_HEADS, HEAD_DIM)
    v_ctx = proj[:N_CTX_TOK, COL_V:COL_V + KV_WIDTH].reshape(BATCH, SEQ, N_KV_HEADS, HEAD_DIM)
    h_ctx = h_ctx.reshape(BATCH, 2, SSM_HEADS, SSM_HEADDIM, SSM_STATE)
    return x1, h2, k_ctx, v_ctx, h_ctx


def _prepare(norm_mix_pre, norm_mix_post, norm_ffn_pre, norm_ffn_post, w_in, w_out, pool_w, pool_scale,
             attn_sink, conv_w, conv_b, dt_bias, a_log, ssm_d, ssm_norm):
    w_dt = jnp.pad(w_in[:, :, PROJ_W:], ((0, 0), (0, 0), (0, LANE - DT_W)))
    return dict(
        g_pre1=_row_params(norm_mix_pre), g_post1=_row_params(norm_mix_post),
        g_pre2=_row_params(norm_ffn_pre), g_post2=_row_params(norm_ffn_post),
        w_in=w_in[:, :, :PROJ_W].astype(bf16), w_dt=w_dt.astype(bf16), w_out=w_out.astype(bf16),
        pool_w=pool_w, pool_scale=_row_params(pool_scale), attn_sink=attn_sink,
        conv_w=conv_w, conv_b=_row_params(conv_b),
        dt_bias=_row_params(dt_bias, LANE), a_log=_row_params(a_log, LANE),
        d_skip=_row_params(jnp.repeat(ssm_d, SSM_HEADDIM, axis=1)), g_ssm=_row_params(ssm_norm),
    )


def kernel(x_prompt, x_sample, cache_k, cache_v, state_ssm, c, c_ctx, w_ada, b_ada, norm_mix_pre,
           norm_mix_post, norm_ffn_pre, norm_ffn_post, w_in, w_out, pool_w, pool_scale, attn_sink,
           conv_w, conv_b, dt_bias, a_log, ssm_d, ssm_norm, ffn_w1, ffn_w3, ffn_w2, router_w, router_b,
           moe_w1, moe_w3, moe_w2):
    prm = _prepare(norm_mix_pre, norm_mix_post, norm_ffn_pre, norm_ffn_post, w_in, w_out, pool_w, pool_scale,
                   attn_sink, conv_w, conv_b, dt_bias, a_log, ssm_d, ssm_norm)
    cond = jnp.concatenate([c_ctx[None, :], c, jnp.zeros((COND_PAD - N_COND, D_MODEL), f32)], axis=0)
    mod = _ada(cond, w_ada, b_ada)
    x = jnp.concatenate([x_prompt.reshape(N_CTX_TOK, D_MODEL), x_sample.reshape(N_LAT_TOK, D_MODEL)], axis=0)
    h0 = state_ssm.reshape(DEC_BATCH, DEPTH, 2, SSD_XW, SSM_STATE)
    ks, vs, hs = [], [], []
    h = _prenorm(x, mod, prm['g_pre1'], 0)
    for l in range(DEPTH):
        dense = l % 2 == 0
        x1, h2, k_l, v_l, h_l = _mix_layer(x, h, mod, l, prm, cache_k, cache_v, h0, bf16 if dense else f32)
        ks.append(k_l)
        vs.append(v_l)
        hs.append(h_l)
        i = l // 2
        if dense:
            f = _ffn(h2, ffn_w1, ffn_w3, ffn_w2, i)
            x, h = _resid(x1, f, mod, prm['g_post2'], prm['g_pre1'], l)
        else:
            y, gates = _moe(h2, router_w[i], router_b[i], moe_w1, moe_w3, moe_w2, i)
            x, h = _resid(x1, y, mod, prm['g_post2'], prm['g_pre1'], l, gates)
    y_p = x[:N_CTX_TOK].reshape(BATCH, SEQ, D_MODEL)
    y_s = x[N_CTX_TOK:].reshape(DEC_BATCH, DEC_SEQ, D_MODEL)
    return (y_p, y_s, jnp.stack(ks, axis=1), jnp.stack(vs, axis=1), jnp.stack(hs, axis=1))
```

```python
import functools

import numpy as np
import jax
import jax.numpy as jnp
from jax import lax
from jax.experimental import pallas as pl
from jax.experimental.pallas import tpu as pltpu

D_MODEL = 2048
BATCH = 16
SEQ = 256
DEPTH = 4
DEC_BATCH = 4
DEC_SEQ = 4096
PAST_LEN = 512
GRID_W = 64
POOL_WIDTH = D_MODEL // 4
ATTN_WIDTH = D_MODEL // 2
SSM_INNER = D_MODEL // 4
POOL_WINDOWS = (2, 4, 8, 16)
POOL_GROUPS = len(POOL_WINDOWS)
POOL_CH = POOL_WIDTH // POOL_GROUPS
HEAD_DIM = 128
N_HEADS = ATTN_WIDTH // HEAD_DIM
N_KV_HEADS = 2
GQA = N_HEADS // N_KV_HEADS
KV_WIDTH = N_KV_HEADS * HEAD_DIM
ATTN_WINDOW = 128
ATTN_BLOCK = 128
ATTN_SPAN = ATTN_BLOCK + 2 * ATTN_WINDOW
ATTN_SCALE = HEAD_DIM ** -0.5
ROPE_PAIRS = HEAD_DIM // 4
ROPE_BASE = 10000.0
SSM_HEADDIM = 64
SSM_HEADS = SSM_INNER // SSM_HEADDIM
SSM_GROUPS = 2
SSM_STATE = 128
SSM_CONV = 5
SSM_CHUNK = 128
CONV_DIM = SSM_INNER + 2 * SSM_GROUPS * SSM_STATE
D_FF = 7168
N_EXPERTS = 8
TOP_K = 2
RMS_EPS = 1e-6

N_CTX_TOK = BATCH * SEQ
N_LAT_TOK = DEC_BATCH * DEC_SEQ
N_TOK = N_CTX_TOK + N_LAT_TOK
N_COND = 1 + DEC_BATCH
COND_PAD = 8
assert N_CTX_TOK == DEC_SEQ

COL_U = 0
COL_Q = COL_U + POOL_WIDTH
COL_K = COL_Q + ATTN_WIDTH
COL_V = COL_K + KV_WIDTH
COL_Z = COL_V + KV_WIDTH
COL_XBC = COL_Z + SSM_INNER
PROJ_W = COL_XBC + CONV_DIM
DT_W = 2 * SSM_HEADS
LANE = 128
HALO = 8

V7X_VMEM_LIMIT = 56 * 1024 * 1024

FFN_TM = 1024
FFN_TF = 256
MOE_TM = 1024
PROJ_TM = 2048
PROJ_TN = 512
OUT_TM = 512
SEQ_TM = SEQ
ADA_TN = 1024

f32 = jnp.float32
bf16 = jnp.bfloat16


def _params(n_axes, **kw):
    return pltpu.CompilerParams(dimension_semantics=("arbitrary",) * n_axes,
                                vmem_limit_bytes=V7X_VMEM_LIMIT, **kw)


def _silu(x):
    return x * jax.nn.sigmoid(x)


def _rms(x):
    return x * lax.rsqrt(jnp.mean(x * x, axis=-1, keepdims=True) + RMS_EPS)


def _cond_row(tm):
    return lambda i: (i * tm) // DEC_SEQ


def _mod_spec(layer, tm, chunk):
    row = _cond_row(tm)
    return pl.BlockSpec((None, None, 1, D_MODEL), lambda i, *_: (layer, row(i), 0, chunk))


def _vec_spec(layer, width=None, col=0):
    return pl.BlockSpec((None, 1, width), lambda *_: (layer, 0, col))


def _ada_kernel(c_ref, w_ref, b_ref, o_ref):
    s = _silu(c_ref[...]).astype(bf16)
    o_ref[...] = jnp.dot(s, w_ref[...].astype(bf16), preferred_element_type=f32) + b_ref[...]


def _ada(cond, w_ada, b_ada):
    n = 6 * D_MODEL
    out = pl.pallas_call(
        _ada_kernel,
        grid=(DEPTH, n // ADA_TN),
        in_specs=[
            pl.BlockSpec((COND_PAD, D_MODEL), lambda l, j: (0, 0)),
            pl.BlockSpec((None, D_MODEL, ADA_TN), lambda l, j: (l, 0, j)),
            pl.BlockSpec((None, 1, ADA_TN), lambda l, j: (l, 0, j)),
        ],
        out_specs=pl.BlockSpec((None, COND_PAD, ADA_TN), lambda l, j: (l, 0, j)),
        out_shape=jax.ShapeDtypeStruct((DEPTH, COND_PAD, n), f32),
        compiler_params=_params(2),
        name="adaln_mod",
    )(cond, w_ada, b_ada.reshape(DEPTH, 1, n))
    return out.reshape(DEPTH, COND_PAD, 1, n)


def _modulated_norm(x, g_ref, sc_ref, sh_ref):
    return (_rms(x) * g_ref[...]) * (1.0 + sc_ref[...]) + sh_ref[...]


PRE_TM = 512


def _prenorm_kernel(x_ref, sh_ref, sc_ref, g_ref, h_ref):
    h_ref[...] = _modulated_norm(x_ref[...], g_ref, sc_ref, sh_ref).astype(bf16)


def _prenorm(x, mod, g_pre, layer):
    tm = PRE_TM
    row = lambda i: (i, 0)
    return pl.pallas_call(
        _prenorm_kernel,
        grid=(N_TOK // tm,),
        in_specs=[pl.BlockSpec((tm, D_MODEL), row), _mod_spec(layer, tm, 0), _mod_spec(layer, tm, 1),
                  _vec_spec(layer, D_MODEL)],
        out_specs=pl.BlockSpec((tm, D_MODEL), row),
        out_shape=jax.ShapeDtypeStruct((N_TOK, D_MODEL), bf16),
        compiler_params=_params(1),
        name="mix_prenorm",
    )(x, mod, mod, g_pre)


def _inproj_kernel(h_ref, w_ref, wdt_ref, o_ref, dt_ref):
    @pl.when(pl.program_id(1) == 0)
    def _():
        dt_ref[...] = jnp.dot(h_ref[...], wdt_ref[...], preferred_element_type=f32)

    o_ref[...] = jnp.dot(h_ref[...], w_ref[...], preferred_element_type=f32)


def _inproj(h, w_main, w_dt, layer):
    tm, tn = PROJ_TM, PROJ_TN
    return pl.pallas_call(
        _inproj_kernel,
        grid=(N_TOK // tm, PROJ_W // tn),
        in_specs=[
            pl.BlockSpec((tm, D_MODEL), lambda i, j: (i, 0)),
            pl.BlockSpec((None, D_MODEL, tn), lambda i, j: (layer, 0, j)),
            pl.BlockSpec((None, D_MODEL, LANE), lambda i, j: (layer, 0, 0)),
        ],
        out_specs=[
            pl.BlockSpec((tm, tn), lambda i, j: (i, j)),
            pl.BlockSpec((tm, LANE), lambda i, j: (i, 0)),
        ],
        out_shape=[jax.ShapeDtypeStruct((N_TOK, PROJ_W), f32),
                   jax.ShapeDtypeStruct((N_TOK, LANE), f32)],
        compiler_params=_params(2),
        name="in_proj",
    )(h, w_main, w_dt)


def _seq_tile_position(i):
    is_ctx = i < N_CTX_TOK // SEQ_TM
    t0 = jnp.where(is_ctx, 0, ((i - N_CTX_TOK // SEQ_TM) * SEQ_TM) % DEC_SEQ)
    seq_len = jnp.where(is_ctx, SEQ, DEC_SEQ)
    return t0, seq_len


def _fill_extended(e_s, cur_ref, prev_ref, next_ref, t0, seq_len):
    first = t0 == 0
    last = t0 + SEQ_TM == seq_len
    e_s[0:HALO, :] = jnp.where(first, 0.0, prev_ref[...])
    e_s[HALO:HALO + SEQ_TM, :] = cur_ref[...]
    e_s[HALO + SEQ_TM:, :] = jnp.where(last, 0.0, next_ref[...])


def _halo_specs(width, col):
    per = SEQ_TM // HALO
    n8 = N_TOK // HALO
    return [
        pl.BlockSpec((SEQ_TM, width), lambda i, *a: (i, col(*a))),
        pl.BlockSpec((HALO, width), lambda i, *a: (jnp.maximum(i * per - 1, 0), col(*a))),
        pl.BlockSpec((HALO, width), lambda i, *a: (jnp.minimum((i + 1) * per, n8 - 1), col(*a))),
    ]


def _pool_kernel(cur_ref, prev_ref, next_ref, w_ref, sc_ref, o_ref, e_s):
    t0, seq_len = _seq_tile_position(pl.program_id(0))
    _fill_extended(e_s, cur_ref, prev_ref, next_ref, t0, seq_len)
    t = t0 + lax.broadcasted_iota(jnp.int32, (SEQ_TM, 1), 0)
    for g, w in enumerate(POOL_WINDOWS):
        lanes = pl.ds(g * POOL_CH, POOL_CH)
        acc = e_s[pl.ds(HALO - w // 2, SEQ_TM), lanes]
        for s in range(1 - w // 2, w // 2):
            acc = acc + e_s[pl.ds(HALO + s, SEQ_TM), lanes]
        cnt = jnp.minimum(t + w // 2, seq_len) - jnp.maximum(t - w // 2, 0)
        mean = acc / cnt.astype(f32)
        d = (mean - e_s[pl.ds(HALO, SEQ_TM), lanes]).astype(bf16)
        y = jnp.dot(d, w_ref[g].astype(bf16), preferred_element_type=f32) * sc_ref[:, g * POOL_CH:(g + 1) * POOL_CH]
        o_ref[:, g * POOL_CH:(g + 1) * POOL_CH] = y.astype(bf16)


def _pool(proj, pool_w, pool_scale, layer):
    return pl.pallas_call(
        _pool_kernel,
        grid=(N_TOK // SEQ_TM,),
        in_specs=_halo_specs(POOL_WIDTH, lambda: COL_U // POOL_WIDTH) + [
            pl.BlockSpec((None, POOL_GROUPS, POOL_CH, POOL_CH), lambda i: (layer, 0, 0, 0)),
            _vec_spec(layer, POOL_WIDTH),
        ],
        out_specs=pl.BlockSpec((SEQ_TM, POOL_WIDTH), lambda i: (i, 0)),
        out_shape=jax.ShapeDtypeStruct((N_TOK, POOL_WIDTH), bf16),
        scratch_shapes=[pltpu.VMEM((SEQ_TM + 2 * HALO, POOL_WIDTH), f32)],
        compiler_params=_params(1),
        name="pool_mixer",
    )(proj, proj, proj, pool_w, pool_scale)


CONV_TN = 512


def _conv_kernel(cur_ref, prev_ref, next_ref, w_ref, b_ref, o_ref, e_s):
    t0, seq_len = _seq_tile_position(pl.program_id(0))
    _fill_extended(e_s, cur_ref, prev_ref, next_ref, t0, seq_len)
    acc = b_ref[...] + e_s[pl.ds(HALO - SSM_CONV // 2, SEQ_TM), :] * w_ref[0:1, :]
    for k in range(1, SSM_CONV):
        acc = acc + e_s[pl.ds(HALO + k - SSM_CONV // 2, SEQ_TM), :] * w_ref[k:k + 1, :]
    o_ref[...] = _silu(acc)


def _conv(proj, conv_w, conv_b, layer):
    col0 = COL_XBC // CONV_TN
    return pl.pallas_call(
        _conv_kernel,
        grid=(N_TOK // SEQ_TM, CONV_DIM // CONV_TN),
        in_specs=_halo_specs(CONV_TN, lambda j: col0 + j) + [
            pl.BlockSpec((None, SSM_CONV, CONV_TN), lambda i, j: (layer, 0, j)),
            pl.BlockSpec((None, 1, CONV_TN), lambda i, j: (layer, 0, j)),
        ],
        out_specs=pl.BlockSpec((SEQ_TM, CONV_TN), lambda i, j: (i, j)),
        out_shape=jax.ShapeDtypeStruct((N_TOK, CONV_DIM), f32),
        scratch_shapes=[pltpu.VMEM((SEQ_TM + 2 * HALO, CONV_TN), f32)],
        compiler_params=_params(2),
        name="ssm_conv",
    )(proj, proj, proj, conv_w, conv_b)


def _softmax_sink(s, sink):
    m = jnp.maximum(jnp.max(s, axis=-1, keepdims=True), sink)
    e = jnp.exp(s - m)
    return e / (jnp.sum(e, axis=-1, keepdims=True) + jnp.exp(sink - m))


def _qk(q, k):
    return lax.dot_general(q, k, (((1,), (1,)), ((), ())), preferred_element_type=f32)


def _ctx_attn_kernel(sink_ref, q_ref, k_ref, v_ref, o_ref):
    kvh = pl.program_id(1)
    k = k_ref[...].astype(bf16)
    v = v_ref[...].astype(bf16)
    for g in range(GQA):
        q = q_ref[:, g * HEAD_DIM:(g + 1) * HEAD_DIM].astype(bf16)
        p = _softmax_sink(_qk(q, k) * ATTN_SCALE, sink_ref[kvh * GQA + g])
        o = jnp.dot(p.astype(bf16), v, preferred_element_type=f32)
        o_ref[:, g * HEAD_DIM:(g + 1) * HEAD_DIM] = o.astype(bf16)


def _ctx_attention(proj, sink):
    qw = GQA * HEAD_DIM
    return pl.pallas_call(
        _ctx_attn_kernel,
        grid=(BATCH, N_KV_HEADS),
        in_specs=[
            pl.BlockSpec(memory_space=pltpu.SMEM),
            pl.BlockSpec((SEQ, qw), lambda b, h: (b, COL_Q // qw + h)),
            pl.BlockSpec((SEQ, HEAD_DIM), lambda b, h: (b, COL_K // HEAD_DIM + h)),
            pl.BlockSpec((SEQ, HEAD_DIM), lambda b, h: (b, COL_V // HEAD_DIM + h)),
        ],
        out_specs=pl.BlockSpec((SEQ, qw), lambda b, h: (b, h)),
        out_shape=jax.ShapeDtypeStruct((N_CTX_TOK, ATTN_WIDTH), bf16),
        compiler_params=_params(2),
        name="ctx_attention",
    )(sink, proj, proj, proj)


def _rope_tables():
    t = np.arange(DEC_SEQ)
    inv = ROPE_BASE ** (-np.arange(ROPE_PAIRS, dtype=np.float64) / ROPE_PAIRS)
    ang_r = (t // GRID_W)[:, None] * inv
    ang_c = (t % GRID_W)[:, None] * inv
    cos = np.concatenate([np.cos(ang_r), np.cos(ang_r), np.cos(ang_c), np.cos(ang_c)], axis=1)
    sin = np.concatenate([-np.sin(ang_r), np.sin(ang_r), -np.sin(ang_c), np.sin(ang_c)], axis=1)
    return jnp.asarray(cos, f32), jnp.asarray(sin, f32)


def _rope(x, cos, sin):
    lane = lax.broadcasted_iota(jnp.int32, x.shape, 1)
    partner = jnp.where(lane % (2 * ROPE_PAIRS) < ROPE_PAIRS,
                        pltpu.roll(x, HEAD_DIM - ROPE_PAIRS, 1), pltpu.roll(x, ROPE_PAIRS, 1))
    return x * cos + partner * sin


ROPE_ROWS = 512


def _lat_attn_kernel(sink_ref, q0_ref, q1_ref, k_ref, v_ref, ck_ref, cv_ref, cosq_ref, sinq_ref,
                     cos_ref, sin_ref, o_ref, k_s, v_s, ck_s, cv_s):
    qb = pl.program_id(1)

    @pl.when(qb == 0)
    def _():
        zeros = jnp.zeros((ATTN_WINDOW, KV_WIDTH), bf16)
        k_s[0:ATTN_WINDOW, :] = zeros
        v_s[0:ATTN_WINDOW, :] = zeros
        k_s[ATTN_WINDOW + DEC_SEQ:, :] = zeros
        v_s[ATTN_WINDOW + DEC_SEQ:, :] = zeros

        def stage(c, carry):
            r = pl.multiple_of(c * ROPE_ROWS, ROPE_ROWS)
            cos = cos_ref[pl.ds(r, ROPE_ROWS), :]
            sin = sin_ref[pl.ds(r, ROPE_ROWS), :]
            for h in range(N_KV_HEADS):
                kh = k_ref[pl.ds(r, ROPE_ROWS), h * HEAD_DIM:(h + 1) * HEAD_DIM]
                k_s[pl.ds(ATTN_WINDOW + r, ROPE_ROWS), h * HEAD_DIM:(h + 1) * HEAD_DIM] = _rope(kh, cos, sin).astype(bf16)
            v_s[pl.ds(ATTN_WINDOW + r, ROPE_ROWS), :] = v_ref[pl.ds(r, ROPE_ROWS), :].astype(bf16)
            return carry

        lax.fori_loop(0, DEC_SEQ // ROPE_ROWS, stage, 0)
        ck_s[...] = ck_ref[...].astype(bf16)
        cv_s[...] = cv_ref[...].astype(bf16)

    start = pl.multiple_of(qb * ATTN_BLOCK, ATTN_BLOCK)
    rows = GQA * ATTN_BLOCK
    a = lax.broadcasted_iota(jnp.int32, (rows, ATTN_SPAN), 0) % ATTN_BLOCK
    b = lax.broadcasted_iota(jnp.int32, (rows, ATTN_SPAN), 1)
    pos = start - ATTN_WINDOW + b
    mask = (b >= a) & (b <= a + 2 * ATTN_WINDOW) & (pos >= 0) & (pos < DEC_SEQ)
    head_of_row = lax.broadcasted_iota(jnp.int32, (rows, 1), 0) // ATTN_BLOCK
    cos = cosq_ref[...]
    sin = sinq_ref[...]
    for h, q_ref in enumerate((q0_ref, q1_ref)):
        q = jnp.concatenate(
            [_rope(q_ref[:, g * HEAD_DIM:(g + 1) * HEAD_DIM], cos, sin) for g in range(GQA)], axis=0).astype(bf16)
        sink = jnp.zeros((rows, 1), f32)
        for g in range(GQA):
            sink = jnp.where(head_of_row == g, sink_ref[h * GQA + g], sink)
        lanes = pl.ds(h * HEAD_DIM, HEAD_DIM)
        kw = k_s[pl.ds(start, ATTN_SPAN), lanes]
        vw = v_s[pl.ds(start, ATTN_SPAN), lanes]
        s_win = jnp.where(mask, _qk(q, kw) * ATTN_SCALE, -jnp.inf)
        s_ctx = _qk(q, ck_s[:, lanes]) * ATTN_SCALE
        m = jnp.maximum(jnp.maximum(jnp.max(s_win, axis=-1, keepdims=True),
                                    jnp.max(s_ctx, axis=-1, keepdims=True)), sink)
        e_win = jnp.exp(s_win - m)
        e_ctx = jnp.exp(s_ctx - m)
        den = (jnp.sum(e_win, axis=-1, keepdims=True) + jnp.sum(e_ctx, axis=-1, keepdims=True)
               + jnp.exp(sink - m))
        o = (jnp.dot((e_win / den).astype(bf16), vw, preferred_element_type=f32)
             + jnp.dot((e_ctx / den).astype(bf16), cv_s[:, lanes], preferred_element_type=f32))
        for g in range(GQA):
            col = (h * GQA + g) * HEAD_DIM
            o_ref[:, col:col + HEAD_DIM] = o[g * ATTN_BLOCK:(g + 1) * ATTN_BLOCK].astype(bf16)


def _lat_attention(proj, cache_k, cache_v, sink, layer):
    cos, sin = _rope_tables()
    qw = GQA * HEAD_DIM
    nqb = DEC_SEQ // ATTN_BLOCK
    row0 = N_CTX_TOK // ATTN_BLOCK
    seq0 = N_CTX_TOK // DEC_SEQ
    ck = cache_k.reshape(DEC_BATCH, DEPTH, PAST_LEN, KV_WIDTH)
    cv = cache_v.reshape(DEC_BATCH, DEPTH, PAST_LEN, KV_WIDTH)
    ctx_spec = pl.BlockSpec((None, None, PAST_LEN, KV_WIDTH), lambda b, n: (b, layer, 0, 0))
    return pl.pallas_call(
        _lat_attn_kernel,
        grid=(DEC_BATCH, nqb),
        in_specs=[
            pl.BlockSpec(memory_space=pltpu.SMEM),
            pl.BlockSpec((ATTN_BLOCK, qw), lambda b, n: (row0 + b * nqb + n, COL_Q // qw)),
            pl.BlockSpec((ATTN_BLOCK, qw), lambda b, n: (row0 + b * nqb + n, COL_Q // qw + 1)),
            pl.BlockSpec((DEC_SEQ, KV_WIDTH), lambda b, n: (seq0 + b, COL_K // KV_WIDTH)),
            pl.BlockSpec((DEC_SEQ, KV_WIDTH), lambda b, n: (seq0 + b, COL_V // KV_WIDTH)),
            ctx_spec, ctx_spec,
            pl.BlockSpec((ATTN_BLOCK, HEAD_DIM), lambda b, n: (n, 0)),
            pl.BlockSpec((ATTN_BLOCK, HEAD_DIM), lambda b, n: (n, 0)),
            pl.BlockSpec((DEC_SEQ, HEAD_DIM), lambda b, n: (0, 0)),
            pl.BlockSpec((DEC_SEQ, HEAD_DIM), lambda b, n: (0, 0)),
        ],
        out_specs=pl.BlockSpec((ATTN_BLOCK, ATTN_WIDTH), lambda b, n: (b * nqb + n, 0)),
        out_shape=jax.ShapeDtypeStruct((N_LAT_TOK, ATTN_WIDTH), bf16),
        scratch_shapes=[
            pltpu.VMEM((DEC_SEQ + 2 * ATTN_WINDOW, KV_WIDTH), bf16),
            pltpu.VMEM((DEC_SEQ + 2 * ATTN_WINDOW, KV_WIDTH), bf16),
            pltpu.VMEM((PAST_LEN, KV_WIDTH), bf16),
            pltpu.VMEM((PAST_LEN, KV_WIDTH), bf16),
        ],
        compiler_params=_params(2),
        name="latent_attention",
    )(sink, proj, proj, proj, proj, ck, cv, cos, sin, cos, sin)


SSD_XW = SSM_HEADS * SSM_HEADDIM
SSD_LW = SSM_HEADS * LANE
PAIR_W = 2 * SSM_HEADDIM
GROUP_W = SSD_XW // SSM_GROUPS


def _split3(x):
    hi = x.astype(bf16)
    r = x - hi.astype(f32)
    mid = r.astype(bf16)
    lo = (r - mid.astype(f32)).astype(bf16)
    return hi, mid, lo


def _dot_exact_rhs(sel, x):
    return sum(jnp.dot(sel, p, preferred_element_type=f32) for p in _split3(x))


def _dot_exact_lhs(x, sel):
    return sum(jnp.dot(p, sel, preferred_element_type=f32) for p in _split3(x))


def _ssd_constants():
    q = SSM_CHUNK
    lower = np.tril(np.ones((q, q), np.float32))
    tri = np.stack([lower, lower.T])
    x_wide = np.zeros((2, LANE, SSD_LW), np.float32)
    x_head = np.zeros((2, LANE, SSD_XW), np.float32)
    for d in range(2):
        for j in range(SSM_HEADS):
            x_wide[d, d * SSM_HEADS + j, j * LANE:(j + 1) * LANE] = 1.0
            x_head[d, d * SSM_HEADS + j, j * SSM_HEADDIM:(j + 1) * SSM_HEADDIM] = 1.0
    return jnp.asarray(tri, bf16), jnp.asarray(x_wide, bf16), jnp.asarray(x_head, bf16)


def _ssd_direction(d, act, dt_raw, h_s, bias, a_neg, tri_ref, xw_ref, xh_ref):
    q = SSM_CHUNK
    dt = jnp.logaddexp(dt_raw + bias, 0.0)
    a = dt * a_neg
    cs = _dot_exact_rhs(tri_ref[d], a)
    cs_wide = _dot_exact_lhs(cs, xw_ref[d])
    cs_head = _dot_exact_lhs(cs, xh_ref[d])
    dt_head = _dot_exact_lhs(dt, xh_ref[d])
    cs_t = cs.T
    edge = cs_head[q - 1:q, :] if d == 0 else cs_head[0:1, :]
    xd = act[:, 0:SSD_XW] * dt_head
    xdb = xd.astype(bf16)
    xdw = (xd * jnp.exp(edge - cs_head)).astype(bf16)
    li = lax.broadcasted_iota(jnp.int32, (q, q), 0)
    si = lax.broadcasted_iota(jnp.int32, (q, q), 1)
    seen = (li >= si) if d == 0 else (li <= si)
    lane = lax.broadcasted_iota(jnp.int32, (q, PAIR_W), 1)
    h_in = h_s[...]
    ys = []
    for g in range(SSM_GROUPS):
        bm = act[:, SSD_XW + g * SSM_STATE:SSD_XW + (g + 1) * SSM_STATE]
        cm = act[:, SSD_XW + (SSM_GROUPS + g) * SSM_STATE:SSD_XW + (SSM_GROUPS + g + 1) * SSM_STATE]
        cmb = cm.astype(bf16)
        cb = _qk(cmb, bm.astype(bf16))
        cols = slice(g * GROUP_W, (g + 1) * GROUP_W)
        y_off = jnp.dot(cmb, h_in[:, cols].astype(bf16), preferred_element_type=f32) * jnp.exp(cs_head[:, cols])
        y_diag = []
        for pr in range(GROUP_W // PAIR_W):
            pcols = slice(g * GROUP_W + pr * PAIR_W, g * GROUP_W + (pr + 1) * PAIR_W)
            halves = []
            for jj in range(2):
                j = (g * GROUP_W + pr * PAIR_W) // SSM_HEADDIM + jj
                row = d * SSM_HEADS + j
                seg = cs_wide[:, j * LANE:(j + 1) * LANE] - cs_t[row:row + 1, :]
                m = (cb * jnp.exp(jnp.where(seen, seg, -jnp.inf))).astype(bf16)
                halves.append(jnp.dot(m, xdb[:, pcols], preferred_element_type=f32))
            y_diag.append(jnp.where(lane < SSM_HEADDIM, halves[0], halves[1]))
        ys.append(jnp.concatenate(y_diag, axis=1) + y_off)
        st = jnp.dot(bm.T.astype(bf16), xdw[:, cols], preferred_element_type=f32)
        h_s[:, cols] = h_in[:, cols] * jnp.exp(edge[:, cols]) + st
    return jnp.concatenate(ys, axis=1)


def _ssd_kernel(*refs, has_h0, emit_state):
    refs = list(refs)
    af_ref, ab_ref, dtf_ref, dtb_ref = refs[:4]
    refs = refs[4:]
    h0_ref = refs.pop(0) if has_h0 else None
    bias_ref, alog_ref, tri_ref, xw_ref, xh_ref = refs[:5]
    refs = refs[5:]
    yf_ref, yb_ref = refs[:2]
    refs = refs[2:]
    hout_ref = refs.pop(0) if emit_state else None
    hf_s, hb_s = refs
    c = pl.program_id(1)
    blocks = SSD_XW // LANE

    @pl.when(c == 0)
    def _():
        for d, h_s in enumerate((hf_s, hb_s)):
            if has_h0:
                for k in range(blocks):
                    h_s[:, k * LANE:(k + 1) * LANE] = h0_ref[d, k * LANE:(k + 1) * LANE, :].T
            else:
                h_s[...] = jnp.zeros_like(h_s)

    lane = lax.broadcasted_iota(jnp.int32, (1, LANE), 1)
    a_neg = jnp.where(lane < DT_W, -jnp.exp(alog_ref[...]), 0.0)
    bias = bias_ref[...]
    yf_ref[...] = _ssd_direction(0, af_ref[...], dtf_ref[...], hf_s, bias, a_neg, tri_ref, xw_ref, xh_ref)
    yb_ref[...] = _ssd_direction(1, ab_ref[...], dtb_ref[...], hb_s, bias, a_neg, tri_ref, xw_ref, xh_ref)

    if emit_state:
        @pl.when(c == pl.num_programs(1) - 1)
        def _():
            for d, h_s in enumerate((hf_s, hb_s)):
                for k in range(blocks):
                    hout_ref[d, k * LANE:(k + 1) * LANE, :] = h_s[:, k * LANE:(k + 1) * LANE].T


def _ssd(act, dt_raw, h0, dt_bias, a_log, layer, *, n_seq, seq_len, row0, emit_state):
    q = SSM_CHUNK
    nc = seq_len // q
    blk0 = row0 // q
    tri, x_wide, x_head = _ssd_constants()
    fwd = lambda s, c: (blk0 + s * nc + c, 0)
    bwd = lambda s, c: (blk0 + s * nc + nc - 1 - c, 0)
    const3 = lambda s, c: (0, 0, 0)
    in_specs = [
        pl.BlockSpec((q, CONV_DIM), fwd), pl.BlockSpec((q, CONV_DIM), bwd),
        pl.BlockSpec((q, LANE), fwd), pl.BlockSpec((q, LANE), bwd),
    ]
    args = [act, act, dt_raw, dt_raw]
    if h0 is not None:
        in_specs.append(pl.BlockSpec((None, None, 2, SSD_XW, SSM_STATE), lambda s, c: (s, layer, 0, 0, 0)))
        args.append(h0)
    in_specs += [
        _vec_spec(layer, LANE), _vec_spec(layer, LANE),
        pl.BlockSpec(tri.shape, const3), pl.BlockSpec(x_wide.shape, const3), pl.BlockSpec(x_head.shape, const3),
    ]
    args += [dt_bias, a_log, tri, x_wide, x_head]
    rows = n_seq * seq_len
    out_specs = [pl.BlockSpec((q, SSD_XW), lambda s, c: (s * nc + c, 0)),
                 pl.BlockSpec((q, SSD_XW), lambda s, c: (s * nc + nc - 1 - c, 0))]
    out_shape = [jax.ShapeDtypeStruct((rows, SSD_XW), f32)] * 2
    if emit_state:
        out_specs.append(pl.BlockSpec((None, 2, SSD_XW, SSM_STATE), lambda s, c: (s, 0, 0, 0)))
        out_shape.append(jax.ShapeDtypeStruct((n_seq, 2, SSD_XW, SSM_STATE), f32))
    return pl.pallas_call(
        functools.partial(_ssd_kernel, has_h0=h0 is not None, emit_state=emit_state),
        grid=(n_seq, nc),
        in_specs=in_specs,
        out_specs=out_specs,
        out_shape=out_shape,
        scratch_shapes=[pltpu.VMEM((SSM_STATE, SSD_XW), f32), pltpu.VMEM((SSM_STATE, SSD_XW), f32)],
        compiler_params=_params(2),
        name="ssd_scan",
    )(*args)


OUT_TN = 512
OUT_NT = D_MODEL // OUT_TN


def _outproj_kernel(yp_ref, oc_ref, ol_ref, yfc_ref, yfl_ref, ybc_ref, ybl_ref, xs_ref, z_ref, x_ref,
                    gate_ref, sh2_ref, sc2_ref, dx_ref, gssm_ref, gpost_ref, gpre2_ref, w_ref,
                    x1_ref, h2_ref, a_s, m_s):
    j = pl.program_id(1)

    @pl.when(j == 0)
    def _():
        is_ctx = pl.program_id(0) < N_CTX_TOK // OUT_TM
        yf = jnp.where(is_ctx, yfc_ref[...], yfl_ref[...])
        yb = jnp.where(is_ctx, ybc_ref[...], ybl_ref[...])
        y = (yf + yb + xs_ref[...] * dx_ref[...]) * _silu(z_ref[...])
        a_s[:, 0:POOL_WIDTH] = yp_ref[...]
        a_s[:, POOL_WIDTH:POOL_WIDTH + ATTN_WIDTH] = jnp.where(is_ctx, oc_ref[...], ol_ref[...])
        a_s[:, POOL_WIDTH + ATTN_WIDTH:] = (_rms(y) * gssm_ref[...]).astype(bf16)

    m_s[j] = jnp.dot(a_s[...], w_ref[...], preferred_element_type=f32)

    @pl.when(j == OUT_NT - 1)
    def _():
        ss = sum(jnp.sum(m_s[k] * m_s[k], axis=-1, keepdims=True) for k in range(OUT_NT))
        r = lax.rsqrt(ss / D_MODEL + RMS_EPS)
        ss1 = jnp.zeros_like(ss)
        for k in range(OUT_NT):
            c = slice(k * OUT_TN, (k + 1) * OUT_TN)
            x1 = x_ref[:, c] + gate_ref[:, c] * ((m_s[k] * r) * gpost_ref[:, c])
            x1_ref[:, c] = x1
            ss1 = ss1 + jnp.sum(x1 * x1, axis=-1, keepdims=True)
        r1 = lax.rsqrt(ss1 / D_MODEL + RMS_EPS)
        for k in range(OUT_NT):
            c = slice(k * OUT_TN, (k + 1) * OUT_TN)
            h2 = ((x1_ref[:, c] * r1) * gpre2_ref[:, c]) * (1.0 + sc2_ref[:, c]) + sh2_ref[:, c]
            h2_ref[:, c] = h2.astype(h2_ref.dtype)


def _outproj(y_pool, o_ctx, o_lat, yf_c, yf_l, yb_c, yb_l, act, proj, x, mod, d_skip, g_ssm, g_post, g_pre2,
             w_out, layer, h2_dtype):
    tm = OUT_TM
    n_ctx = N_CTX_TOK // tm
    row = lambda i, j: (i, 0)
    ctx = lambda i, j: (jnp.minimum(i, n_ctx - 1), 0)
    lat = lambda i, j: (jnp.maximum(i - n_ctx, 0), 0)
    return pl.pallas_call(
        _outproj_kernel,
        grid=(N_TOK // tm, OUT_NT),
        in_specs=[
            pl.BlockSpec((tm, POOL_WIDTH), row),
            pl.BlockSpec((tm, ATTN_WIDTH), ctx), pl.BlockSpec((tm, ATTN_WIDTH), lat),
            pl.BlockSpec((tm, SSM_INNER), ctx), pl.BlockSpec((tm, SSM_INNER), lat),
            pl.BlockSpec((tm, SSM_INNER), ctx), pl.BlockSpec((tm, SSM_INNER), lat),
            pl.BlockSpec((tm, SSM_INNER), row),
            pl.BlockSpec((tm, SSM_INNER), lambda i, j: (i, COL_Z // SSM_INNER)),
            pl.BlockSpec((tm, D_MODEL), row),
            _mod_spec(layer, tm, 2), _mod_spec(layer, tm, 3), _mod_spec(layer, tm, 4),
            _vec_spec(layer, SSM_INNER), _vec_spec(layer, SSM_INNER),
            _vec_spec(layer, D_MODEL), _vec_spec(layer, D_MODEL),
            pl.BlockSpec((None, D_MODEL, OUT_TN), lambda i, j: (layer, 0, j)),
        ],
        out_specs=[pl.BlockSpec((tm, D_MODEL), row), pl.BlockSpec((tm, D_MODEL), row)],
        out_shape=[jax.ShapeDtypeStruct((N_TOK, D_MODEL), f32), jax.ShapeDtypeStruct((N_TOK, D_MODEL), h2_dtype)],
        scratch_shapes=[pltpu.VMEM((tm, D_MODEL), bf16), pltpu.VMEM((OUT_NT, tm, OUT_TN), f32)],
        compiler_params=_params(2),
        name="out_proj",
    )(y_pool, o_ctx, o_lat, yf_c, yf_l, yb_c, yb_l, act, proj, x, mod, mod, mod, d_skip, g_ssm, g_post, g_pre2, w_out)


def _swiglu_step(h, w1_ref, w3_ref, w2_ref):
    a = jnp.dot(h, w1_ref[...].astype(bf16), preferred_element_type=f32)
    b = jnp.dot(h, w3_ref[...].astype(bf16), preferred_element_type=f32)
    g = (_silu(a) * b).astype(bf16)
    return jnp.dot(g, w2_ref[...].astype(bf16), preferred_element_type=f32)


def _ffn_kernel(h_ref, w1_ref, w3_ref, w2_ref, o_ref):
    @pl.when(pl.program_id(1) == 0)
    def _():
        o_ref[...] = jnp.zeros_like(o_ref)

    o_ref[...] += _swiglu_step(h_ref[...], w1_ref, w3_ref, w2_ref)


def _ffn(h, w1, w3, w2, n):
    T, D = h.shape
    F = w1.shape[2]
    return pl.pallas_call(
        _ffn_kernel,
        grid=(T // FFN_TM, F // FFN_TF),
        in_specs=[
            pl.BlockSpec((FFN_TM, D), lambda i, j: (i, 0)),
            pl.BlockSpec((None, D, FFN_TF), lambda i, j: (n, 0, j)),
            pl.BlockSpec((None, D, FFN_TF), lambda i, j: (n, 0, j)),
            pl.BlockSpec((None, FFN_TF, D), lambda i, j: (n, j, 0)),
        ],
        out_specs=pl.BlockSpec((FFN_TM, D), lambda i, j: (i, 0)),
        out_shape=jax.ShapeDtypeStruct((T, D), f32),
        compiler_params=_params(2),
        name="ffn_swiglu",
    )(h, w1, w3, w2)


MOE_TK = N_TOK * TOP_K
MOE_BLOCKS = MOE_TK // MOE_TM + N_EXPERTS
MOE_ROWS = MOE_BLOCKS * MOE_TM
SLOT_BITS = 16
assert MOE_TK + MOE_TM <= 1 << SLOT_BITS and N_TOK << SLOT_BITS < 1 << 31
DMA_UNROLL = 8
MOE_SUB = 256


def _moe_kernel(be_ref, nv_ref, nq_ref, idx_ref, h_hbm, w1_ref, w3_ref, w2_ref, y_hbm, xf_s, xb_s, acc_s, sem):
    i = pl.program_id(0)
    j = pl.program_id(1)
    base = i * MOE_TM

    @pl.when(jnp.logical_and(i == 0, j == 0))
    def _():
        acc_s[...] = jnp.zeros_like(acc_s)
        spill = pltpu.make_async_copy(acc_s, y_hbm.at[pl.ds(MOE_TK, MOE_TM), :], sem.at[1])
        spill.start()
        spill.wait()

    groups = nq_ref[i]

    @pl.when(groups > 0)
    def _():
        @pl.when(j == 0)
        def _():
            def gather(r, carry):
                tok = lax.shift_right_logical(idx_ref[base + r], SLOT_BITS)
                pltpu.make_async_copy(h_hbm.at[pl.ds(tok, 1), :], xf_s.at[pl.ds(r, 1), :], sem.at[0]).start()
                return carry

            lax.fori_loop(0, MOE_TM, gather, 0, unroll=DMA_UNROLL)
            pltpu.make_async_copy(h_hbm.at[pl.ds(0, MOE_TM), :], xf_s, sem.at[0]).wait()
            xb_s[...] = xf_s[...].astype(bf16)
            acc_s[...] = jnp.zeros_like(acc_s)

        for q in range(1, MOE_TM // MOE_SUB + 1):
            @pl.when(groups == q)
            def _(rows=q * MOE_SUB):
                acc_s[0:rows, :] += _swiglu_step(xb_s[0:rows, :], w1_ref, w3_ref, w2_ref)

        @pl.when(j == pl.num_programs(1) - 1)
        def _():
            def scatter(r, carry):
                slot = idx_ref[base + r] & ((1 << SLOT_BITS) - 1)
                pltpu.make_async_copy(acc_s.at[pl.ds(r, 1), :], y_hbm.at[pl.ds(slot, 1), :], sem.at[1]).start()
                return carry

            lax.fori_loop(0, MOE_TM, scatter, 0, unroll=DMA_UNROLL)
            pltpu.make_async_copy(acc_s, y_hbm.at[pl.ds(0, MOE_TM), :], sem.at[1]).wait()


def _moe_experts(row_idx, block_e, n_valid, groups, h, w1, w3, w2, n):
    D = D_MODEL
    nf = D_FF // FFN_TF

    def jeff(i, j, nv):
        return jnp.where(i < nv[0], j, nf - 1)

    return pl.pallas_call(
        _moe_kernel,
        grid_spec=pltpu.PrefetchScalarGridSpec(
            num_scalar_prefetch=4,
            grid=(MOE_BLOCKS, nf),
            in_specs=[
                pl.BlockSpec(memory_space=pl.ANY),
                pl.BlockSpec((None, None, D, FFN_TF), lambda i, j, be, nv, nq, ix: (n, be[i], 0, jeff(i, j, nv))),
                pl.BlockSpec((None, None, D, FFN_TF), lambda i, j, be, nv, nq, ix: (n, be[i], 0, jeff(i, j, nv))),
                pl.BlockSpec((None, None, FFN_TF, D), lambda i, j, be, nv, nq, ix: (n, be[i], jeff(i, j, nv), 0)),
            ],
            out_specs=pl.BlockSpec(memory_space=pl.ANY),
            scratch_shapes=[pltpu.VMEM((MOE_TM, D), f32), pltpu.VMEM((MOE_TM, D), bf16),
                            pltpu.VMEM((MOE_TM, D), f32), pltpu.SemaphoreType.DMA((2,))],
        ),
        out_shape=jax.ShapeDtypeStruct((MOE_TK + MOE_TM, D), f32),
        compiler_params=_params(2, disable_bounds_checks=True),
        name="moe_swiglu",
    )(block_e, n_valid, groups, row_idx, h, w1, w3, w2)


ROUTE_TM = 1024


def _router_kernel(h_ref, whi_ref, wlo_ref, b_ref, e_ref, g_ref):
    h = h_ref[...]
    h_hi = h.astype(bf16)
    h_lo = (h - h_hi.astype(f32)).astype(bf16)
    logits = (jnp.dot(h_hi, whi_ref[...], preferred_element_type=f32)
              + (jnp.dot(h_lo, whi_ref[...], preferred_element_type=f32)
                 + jnp.dot(h_hi, wlo_ref[...], preferred_element_type=f32))) + b_ref[...]
    lane = lax.broadcasted_iota(jnp.int32, logits.shape, 1)
    m1 = jnp.max(logits, axis=-1, keepdims=True)
    i1 = jnp.min(jnp.where(logits == m1, lane, LANE), axis=-1, keepdims=True)
    rest = jnp.where(lane == i1, -jnp.inf, logits)
    m2 = jnp.max(rest, axis=-1, keepdims=True)
    i2 = jnp.min(jnp.where(rest == m2, lane, LANE), axis=-1, keepdims=True)
    e2 = jnp.exp(m2 - m1)
    g1 = 1.0 / (1.0 + e2)
    e_ref[...] = jnp.where(lane == 0, i1, jnp.where(lane == 1, i2, 0))
    g_ref[...] = jnp.where(lane == 0, g1, jnp.where(lane == 1, e2 * g1, 0.0))


def _router(h, router_w, router_b):
    w = jnp.pad(router_w, ((0, 0), (0, LANE - N_EXPERTS)))
    w_hi = w.astype(bf16)
    w_lo = (w - w_hi.astype(f32)).astype(bf16)
    b = jnp.pad(router_b, (0, LANE - N_EXPERTS), constant_values=-jnp.inf)[None, :]
    tm = ROUTE_TM
    return pl.pallas_call(
        _router_kernel,
        grid=(N_TOK // tm,),
        in_specs=[pl.BlockSpec((tm, D_MODEL), lambda i: (i, 0)),
                  pl.BlockSpec((D_MODEL, LANE), lambda i: (0, 0)),
                  pl.BlockSpec((D_MODEL, LANE), lambda i: (0, 0)),
                  pl.BlockSpec((1, LANE), lambda i: (0, 0))],
        out_specs=[pl.BlockSpec((tm, LANE), lambda i: (i, 0)), pl.BlockSpec((tm, LANE), lambda i: (i, 0))],
        out_shape=[jax.ShapeDtypeStruct((N_TOK, LANE), jnp.int32), jax.ShapeDtypeStruct((N_TOK, LANE), f32)],
        compiler_params=_params(1),
        name="moe_router",
    )(h, w_hi, w_lo, b)


def _moe_plan(top_e):
    flat_e = top_e.reshape(-1)
    onehot = (flat_e[:, None] == jnp.arange(N_EXPERTS, dtype=jnp.int32)[None, :]).astype(jnp.int32)
    rank = jnp.cumsum(onehot, axis=0) - onehot
    counts = jnp.sum(onehot, axis=0)
    padded = (counts + MOE_TM - 1) // MOE_TM * MOE_TM
    pad_ends = jnp.cumsum(padded)
    pad_starts = pad_ends - padded
    dest = jnp.sum(onehot * (pad_starts[None, :] + rank), axis=1)
    pair = jnp.arange(MOE_TK, dtype=jnp.int32)
    packed = ((pair // TOP_K) << SLOT_BITS) | ((pair % TOP_K) * N_TOK + pair // TOP_K)
    spill = MOE_TK + jnp.arange(MOE_ROWS, dtype=jnp.int32) % MOE_TM
    row_idx = spill.at[dest].set(packed, unique_indices=True)
    n_valid = (pad_ends[-1] // MOE_TM).astype(jnp.int32)
    blocks = jnp.arange(MOE_BLOCKS, dtype=jnp.int32)
    block_e = jnp.sum((blocks[:, None] * MOE_TM >= pad_ends[None, :]).astype(jnp.int32), axis=1)
    block_e = jnp.minimum(block_e, N_EXPERTS - 1)
    real_rows = jnp.clip((pad_starts + counts)[block_e] - blocks * MOE_TM, 0, MOE_TM)
    groups = jnp.where(blocks < n_valid, (real_rows + MOE_SUB - 1) // MOE_SUB, 0)
    last_e = block_e[jnp.maximum(n_valid - 1, 0)]
    block_e = jnp.where(blocks < n_valid, block_e, last_e)
    return row_idx, block_e, n_valid.reshape(1), groups


def _moe(h, router_w, router_b, w1, w3, w2, n):
    top_e, gates = _router(h, router_w, router_b)
    row_idx, block_e, n_valid, groups = _moe_plan(top_e[:, :TOP_K])
    return _moe_experts(row_idx, block_e, n_valid, groups, h, w1, w3, w2, n), gates


RES_TM = 512


def _resid_kernel(x_ref, *refs, n_f, emit_next):
    f_refs, refs = refs[:n_f], refs[n_f:]
    if n_f == 1:
        f = f_refs[0][...]
    else:
        gw_ref, refs = refs[0], refs[1:]
        f = gw_ref[:, 0:1] * f_refs[0][...]
        for k in range(1, n_f):
            f = f + gw_ref[:, k:k + 1] * f_refs[k][...]
    gate_ref, g_ref = refs[:2]
    x2 = x_ref[...] + gate_ref[...] * (_rms(f) * g_ref[...])
    if emit_next:
        sh_ref, sc_ref, gpre_ref, o_ref, h_ref = refs[2:]
        h_ref[...] = _modulated_norm(x2, gpre_ref, sc_ref, sh_ref).astype(bf16)
    else:
        o_ref, = refs[2:]
    o_ref[...] = x2


def _resid(x, f, mod, g_post, g_pre1, layer, gates=None):
    tm = RES_TM
    row = lambda i: (i, 0)
    tile = pl.BlockSpec((tm, D_MODEL), row)
    emit_next = layer + 1 < DEPTH
    if gates is None:
        n_f, args, in_specs = 1, [x, f], [tile, tile]
    else:
        n_f, args = TOP_K, [x] + [f] * TOP_K + [gates]
        choice = [pl.BlockSpec((tm, D_MODEL), functools.partial(lambda k, i: (k * (N_TOK // tm) + i, 0), k))
                  for k in range(TOP_K)]
        in_specs = [tile] + choice + [pl.BlockSpec((tm, LANE), row)]
    args += [mod, g_post]
    in_specs += [_mod_spec(layer, tm, 5), _vec_spec(layer, D_MODEL)]
    out_specs, out_shape = [tile], [jax.ShapeDtypeStruct((N_TOK, D_MODEL), f32)]
    if emit_next:
        args += [mod, mod, g_pre1]
        in_specs += [_mod_spec(layer + 1, tm, 0), _mod_spec(layer + 1, tm, 1), _vec_spec(layer + 1, D_MODEL)]
        out_specs.append(tile)
        out_shape.append(jax.ShapeDtypeStruct((N_TOK, D_MODEL), bf16))
    out = pl.pallas_call(
        functools.partial(_resid_kernel, n_f=n_f, emit_next=emit_next),
        grid=(N_TOK // tm,),
        in_specs=in_specs,
        out_specs=out_specs,
        out_shape=out_shape,
        compiler_params=_params(1),
        name="ffn_residual",
    )(*args)
    return (out[0], out[1]) if emit_next else (out[0], None)


def _row_params(p, width=None):
    p = p.reshape(DEPTH, -1)
    if width is not None and p.shape[1] < width:
        p = jnp.pad(p, ((0, 0), (0, width - p.shape[1])))
    return p[:, None, :]


def _mix_layer(x, h, mod, layer, prm, cache_k, cache_v, state_ssm, h2_dtype=bf16):
    proj, dt_raw = _inproj(h, prm['w_in'], prm['w_dt'], layer)
    y_pool = _pool(proj, prm['pool_w'], prm['pool_scale'], layer)
    act = _conv(proj, prm['conv_w'], prm['conv_b'], layer)
    sink = prm['attn_sink'][layer]
    o_ctx = _ctx_attention(proj, sink)
    o_lat = _lat_attention(proj, cache_k, cache_v, sink, layer)
    yf_c, yb_c, h_ctx = _ssd(act, dt_raw, None, prm['dt_bias'], prm['a_log'], layer,
                             n_seq=BATCH, seq_len=SEQ, row0=0, emit_state=True)
    yf_l, yb_l = _ssd(act, dt_raw, state_ssm, prm['dt_bias'], prm['a_log'], layer,
                      n_seq=DEC_BATCH, seq_len=DEC_SEQ, row0=N_CTX_TOK, emit_state=False)
    x1, h2 = _outproj(y_pool, o_ctx, o_lat, yf_c, yf_l, yb_c, yb_l, act, proj, x, mod, prm['d_skip'],
                      prm['g_ssm'], prm['g_post1'], prm['g_pre2'], prm['w_out'], layer, h2_dtype)
    k_ctx = proj[:N_CTX_TOK, COL_K:COL_K + KV_WIDTH].reshape(BATCH, SEQ, N_KV_HEADS, HEAD_DIM)
    v_ctx = proj[:N_CTX_TOK, COL_V:COL_V + KV_WIDTH].reshape(BATCH, SEQ, N_KV_HEADS, HEAD_DIM)
    h_ctx = h_ctx.reshape(BATCH, 2, SSM_HEADS, SSM_HEADDIM, SSM_STATE)
    return x1, h2, k_ctx, v_ctx, h_ctx


def _prepare(norm_mix_pre, norm_mix_post, norm_ffn_pre, norm_ffn_post, w_in, w_out, pool_w, pool_scale,
             attn_sink, conv_w, conv_b, dt_bias, a_log, ssm_d, ssm_norm):
    w_dt = jnp.pad(w_in[:, :, PROJ_W:], ((0, 0), (0, 0), (0, LANE - DT_W)))
    return dict(
        g_pre1=_row_params(norm_mix_pre), g_post1=_row_params(norm_mix_post),
        g_pre2=_row_params(norm_ffn_pre), g_post2=_row_params(norm_ffn_post),
        w_in=w_in[:, :, :PROJ_W].astype(bf16), w_dt=w_dt.astype(bf16), w_out=w_out.astype(bf16),
        pool_w=pool_w, pool_scale=_row_params(pool_scale), attn_sink=attn_sink,
        conv_w=conv_w, conv_b=_row_params(conv_b),
        dt_bias=_row_params(dt_bias, LANE), a_log=_row_params(a_log, LANE),
        d_skip=_row_params(jnp.repeat(ssm_d, SSM_HEADDIM, axis=1)), g_ssm=_row_params(ssm_norm),
    )


def kernel(x_prompt, x_sample, cache_k, cache_v, state_ssm, c, c_ctx, w_ada, b_ada, norm_mix_pre,
           norm_mix_post, norm_ffn_pre, norm_ffn_post, w_in, w_out, pool_w, pool_scale, attn_sink,
           conv_w, conv_b, dt_bias, a_log, ssm_d, ssm_norm, ffn_w1, ffn_w3, ffn_w2, router_w, router_b,
           moe_w1, moe_w3, moe_w2):
    prm = _prepare(norm_mix_pre, norm_mix_post, norm_ffn_pre, norm_ffn_post, w_in, w_out, pool_w, pool_scale,
                   attn_sink, conv_w, conv_b, dt_bias, a_log, ssm_d, ssm_norm)
    cond = jnp.concatenate([c_ctx[None, :], c, jnp.zeros((COND_PAD - N_COND, D_MODEL), f32)], axis=0)
    mod = _ada(cond, w_ada, b_ada)
    x = jnp.concatenate([x_prompt.reshape(N_CTX_TOK, D_MODEL), x_sample.reshape(N_LAT_TOK, D_MODEL)], axis=0)
    h0 = state_ssm.reshape(DEC_BATCH, DEPTH, 2, SSD_XW, SSM_STATE)
    ks, vs, hs = [], [], []
    h = _prenorm(x, mod, prm['g_pre1'], 0)
    for l in range(DEPTH):
        dense = l % 2 == 0
        x1, h2, k_l, v_l, h_l = _mix_layer(x, h, mod, l, prm, cache_k, cache_v, h0, bf16 if dense else f32)
        ks.append(k_l)
        vs.append(v_l)
        hs.append(h_l)
        i = l // 2
        if dense:
            f = _ffn(h2, ffn_w1, ffn_w3, ffn_w2, i)
            x, h = _resid(x1, f, mod, prm['g_post2'], prm['g_pre1'], l)
        else:
            y, gates = _moe(h2, router_w[i], router_b[i], moe_w1, moe_w3, moe_w2, i)
            x, h = _resid(x1, y, mod, prm['g_post2'], prm['g_pre1'], l, gates)
    y_p = x[:N_CTX_TOK].reshape(BATCH, SEQ, D_MODEL)
    y_s = x[N_CTX_TOK:].reshape(DEC_BATCH, DEC_SEQ, D_MODEL)
    return (y_p, y_s, jnp.stack(ks, axis=1), jnp.stack(vs, axis=1), jnp.stack(hs, axis=1))
```
